```python
import math
import jax, jax.numpy as jnp
from jax import lax
import numpy as np

D_MODEL = 4096
BATCH = 4
SEQ = 2048
DEPTH = 4
DEC_BATCH = 128
DEC_SEQ = 1
PAST_LEN = 16384
PAGE_SIZE = 128

N_MIXERS = 3
N_A = (DEPTH + 2) // 3
N_B = (DEPTH + 1) // 3
N_C = DEPTH // 3
N_DENSE = (DEPTH + 1) // 2
N_MOE = DEPTH // 2
NORM_EPS = 1e-6

A_HEAD = 64
A_HEADS = D_MODEL // A_HEAD
A_DECAY_LORA = max(32, int(round(1.8 * D_MODEL ** 0.5 / 32)) * 32)
A_AAA_LORA = max(32, int(round(1.8 * D_MODEL ** 0.5 / 32)) * 32)
A_MV_LORA = max(32, int(round(1.3 * D_MODEL ** 0.5 / 32)) * 32)
A_GATE_LORA = max(32, int(round(0.6 * D_MODEL ** 0.8 / 32)) * 32)
A_GN_EPS = 64e-5

B_HEADS = 16
B_QK = D_MODEL // B_HEADS
B_V = 2 * D_MODEL // B_HEADS
B_CHUNK = 64
B_GN_EPS = 1e-6
ROPE_BASE = 10000.0

C_HEAD = 128
C_HEADS = D_MODEL // C_HEAD
C_CONV = 4
C_CHUNK = 64

D_FF = 256 * math.ceil(8 * D_MODEL / 3 / 256)
N_EXPERTS = 8
TOP_K = 2
D_FF_EXPERT = 7 * D_MODEL // 4

kernel_name = 'hybrid_rwkv7_retnet_gdn_adaln_step'

F32 = jnp.float32


def rms_norm(x, g):
    xf = x.astype(F32)
    y = xf * lax.rsqrt(jnp.mean(jnp.square(xf), axis=-1, keepdims=True) + NORM_EPS)
    return (y * g.astype(F32)).astype(x.dtype)


def head_norm(y, eps):
    mu = jnp.mean(y, axis=-1, keepdims=True)
    var = jnp.mean(jnp.square(y - mu), axis=-1, keepdims=True)
    return (y - mu) * lax.rsqrt(var + eps)


def adaln(c, w_ada, b_ada):
    m = (jax.nn.silu(c) @ w_ada + b_ada).reshape(c.shape[0], 6, 1, D_MODEL)
    return [m[:, s] for s in range(6)]


def rotary(x, pos):
    d = x.shape[-1]
    inv = ROPE_BASE ** (-jnp.arange(0, d, 2, dtype=F32) / d)
    ang = pos.astype(F32)[:, None] * inv[None, :]
    cos = jnp.cos(ang)[None, :, None, :]
    sin = jnp.sin(ang)[None, :, None, :]
    x = x.astype(F32)
    x1, x2 = x[..., : d // 2], x[..., d // 2:]
    return jnp.concatenate([x1 * cos - x2 * sin, x1 * sin + x2 * cos], axis=-1)


def to_chunks(t, n, c):
    B, _, H = t.shape[:3]
    return jnp.moveaxis(t.reshape(B, n, c, H, -1), (1, 3), (0, 2))


def from_chunks(t):
    n, B, H, c, d = t.shape
    return jnp.moveaxis(t, (0, 2), (1, 3)).reshape(B, n * c, H, d)


def wkv7_scan(r, w, k, v, a, b, s0):
    seq = tuple(jnp.moveaxis(t, 1, 0) for t in (r, w, k, v, a, b))

    def step(S, inp):
        r_t, w_t, k_t, v_t, a_t, b_t = inp
        sa = jnp.einsum('bhij,bhj->bhi', S, a_t)
        S = S * w_t[:, :, None, :] + sa[..., None] * b_t[:, :, None, :] + v_t[..., None] * k_t[:, :, None, :]
        return S, jnp.einsum('bhij,bhj->bhi', S, r_t)

    S, y = lax.scan(step, s0, seq)
    return S, jnp.moveaxis(y, 0, 1)


def rwkv7_time_mix(h, shift_prev, s0, v_first, P, j):
    B, L, _ = h.shape
    H, N = A_HEADS, A_HEAD
    prev = jnp.concatenate([shift_prev[:, None, :].astype(h.dtype), h[:, :-1]], axis=1)
    mixed = h[None] + (prev - h)[None] * P['rwkv_mu'][j][:, None, None, :]
    r, k, v = jnp.einsum('sbld,sde->sble', mixed[:3], P['rwkv_w_rkv'][j])
    xv, xw, xa, xg = mixed[2], mixed[3], mixed[4], mixed[5]
    w_log = -jax.nn.softplus(-(P['rwkv_w0'][j] + jnp.tanh(xw @ P['rwkv_w1'][j]) @ P['rwkv_w2'][j])) - 0.5
    decay = jnp.exp(-jnp.exp(w_log.astype(F32)))
    if v_first is None:
        v_first = v
    else:
        mix = jax.nn.sigmoid(P['rwkv_v0'][j - 1] + (xv @ P['rwkv_v1'][j - 1]) @ P['rwkv_v2'][j - 1])
        v = v + (v_first - v) * mix
    a = jax.nn.sigmoid(P['rwkv_a0'][j] + (xa @ P['rwkv_a1'][j]) @ P['rwkv_a2'][j])
    g = jax.nn.sigmoid(xg @ P['rwkv_g1'][j]) @ P['rwkv_g2'][j]

    def heads(t):
        return t.astype(F32).reshape(B, L, H, N)

    kk = heads(k * P['rwkv_k_k'][j])
    kk = kk / jnp.maximum(jnp.sqrt(jnp.sum(jnp.square(kk), axis=-1, keepdims=True)), 1e-12)
    k = k * (1 + (a - 1) * P['rwkv_k_a'][j])
    rh, kh, vh, ah = heads(r), heads(k), heads(v), heads(a)
    s_new, y = wkv7_scan(rh, decay.reshape(B, L, H, N), kh, vh, -kk, kk * ah, s0.astype(F32))
    y = head_norm(y, A_GN_EPS).reshape(B, L, D_MODEL) * P['rwkv_ln_w'][j] + P['rwkv_ln_b'][j]
    bonus = jnp.sum(rh * kh * P['rwkv_r_k'][j], axis=-1, keepdims=True) * vh
    out = ((y + bonus.reshape(B, L, D_MODEL)) * g.astype(F32)).astype(h.dtype) @ P['rwkv_w_o'][j]
    return out, h[:, -1].astype(shift_prev.dtype), s_new.astype(s0.dtype), v_first


def retention(h, s0, pos0, P, j):
    B, L, _ = h.shape
    H, dk, dv = B_HEADS, B_QK, B_V
    proj = h @ P['ret_w_in'][j]
    q, k, v, g = jnp.split(proj, [D_MODEL, 2 * D_MODEL, 4 * D_MODEL], axis=-1)
    pos = pos0 + jnp.arange(L)
    q = rotary(q.reshape(B, L, H, dk), pos)
    k = rotary(k.reshape(B, L, H, dk), pos) * dk ** -0.5
    v = v.astype(F32).reshape(B, L, H, dv)
    lg = jnp.log1p(-jnp.exp2(-5.0 - jnp.arange(H, dtype=F32)))
    C = math.gcd(L, B_CHUNK)
    n = L // C
    idx = jnp.arange(C, dtype=F32)
    rel = idx[:, None] - idx[None, :]
    inner = jnp.where(rel >= 0, jnp.exp(lg[:, None, None] * jnp.maximum(rel, 0.0)), 0.0)
    q_dec = jnp.exp(lg[:, None] * (idx + 1.0))[:, :, None]
    k_dec = jnp.exp(lg[:, None] * (C - 1.0 - idx))[:, :, None]
    c_dec = jnp.exp(lg * C)[:, None, None]

    def step(S, inp):
        q_c, k_c, v_c = inp
        att = jnp.einsum('bhid,bhjd->bhij', q_c, k_c) * inner
        o = jnp.einsum('bhij,bhjv->bhiv', att, v_c) + jnp.einsum('bhid,bhdv->bhiv', q_c * q_dec, S)
        S = S * c_dec + jnp.einsum('bhjd,bhjv->bhdv', k_c * k_dec, v_c)
        return S, o

    S, o = lax.scan(step, s0.astype(F32), (to_chunks(q, n, C), to_chunks(k, n, C), to_chunks(v, n, C)))
    o = head_norm(from_chunks(o), B_GN_EPS).reshape(B, L, 2 * D_MODEL)
    out = (jax.nn.silu(g.astype(F32)) * o).astype(h.dtype) @ P['ret_w_o'][j]
    return out, S.astype(s0.dtype)


def causal_depthwise_conv(cat, w):
    return lax.conv_general_dilated(cat, w[:, None, :].astype(cat.dtype), window_strides=(1,), padding='VALID',
                                    dimension_numbers=('NWC', 'WIO', 'NWC'), feature_group_count=cat.shape[-1])


def l2norm(x):
    return x * lax.rsqrt(jnp.sum(jnp.square(x), axis=-1, keepdims=True) + 1e-6)


def chunk_gated_delta(q, k, v, g, beta, s0):
    B, L, H, _ = q.shape
    C = math.gcd(L, C_CHUNK)
    n = L // C
    qc, kc, vc = to_chunks(q, n, C), to_chunks(k, n, C), to_chunks(v, n, C)
    G = jnp.cumsum(to_chunks(g[..., None], n, C)[..., 0], axis=-1)
    bc = to_chunks(beta[..., None], n, C)[..., 0]
    idx = jnp.arange(C)
    incl = idx[:, None] >= idx[None, :]
    strict = idx[:, None] > idx[None, :]
    decay = jnp.exp(jnp.where(incl, G[..., :, None] - G[..., None, :], -jnp.inf))
    kb = kc * bc[..., None]
    M = jnp.where(strict, jnp.einsum('nbhid,nbhjd->nbhij', kb, kc) * decay, 0.0)
    eye = jnp.eye(C, dtype=F32)
    T = lax.linalg.triangular_solve(M + eye, jnp.broadcast_to(eye, M.shape), left_side=True, lower=True,
                                    unit_diagonal=True)
    eG = jnp.exp(G)
    u = T @ (vc * bc[..., None])
    w = T @ (kb * eG[..., None])
    attn = jnp.einsum('nbhid,nbhjd->nbhij', qc, kc) * decay
    qd = qc * eG[..., None]
    kd = kc * jnp.exp(G[..., -1:] - G)[..., None]
    gl = jnp.exp(G[..., -1])[..., None, None]

    def step(S, inp):
        u_c, w_c, qd_c, a_c, kd_c, gl_c = inp
        v_new = u_c - jnp.einsum('bhcd,bhdv->bhcv', w_c, S)
        o = jnp.einsum('bhcd,bhdv->bhcv', qd_c, S) + jnp.einsum('bhij,bhjv->bhiv', a_c, v_new)
        S = S * gl_c + jnp.einsum('bhcd,bhcv->bhdv', kd_c, v_new)
        return S, o

    S, o = lax.scan(step, s0, (u, w, qd, attn, kd, gl))
    return from_chunks(o), S


def gated_deltanet(h, conv_buf, s0, P, j):
    B, L, _ = h.shape
    H, dh = C_HEADS, C_HEAD
    proj = h @ P['gdn_w_in'][j]
    qkv, z, a, b = jnp.split(proj, [3 * D_MODEL, 4 * D_MODEL, 4 * D_MODEL + H], axis=-1)
    cat = jnp.concatenate([conv_buf.astype(h.dtype), qkv], axis=1)
    new_buf = cat[:, -(C_CONV - 1):]
    qkv = jax.nn.silu(causal_depthwise_conv(cat, P['gdn_conv_w'][j]))
    q, k, v = (t.astype(F32).reshape(B, L, H, dh) for t in jnp.split(qkv, 3, axis=-1))
    q = l2norm(q) * dh ** -0.5
    k = l2norm(k)
    beta = jax.nn.sigmoid(b.astype(F32))
    g = -jnp.exp(P['gdn_a_log'][j].astype(F32)) * jax.nn.softplus(a.astype(F32) + P['gdn_dt_bias'][j])
    o, s_new = chunk_gated_delta(q, k, v, g, beta, s0.astype(F32))
    o = o * lax.rsqrt(jnp.mean(jnp.square(o), axis=-1, keepdims=True) + NORM_EPS) * P['gdn_norm_g'][j]
    o = o * jax.nn.silu(z.astype(F32).reshape(B, L, H, dh))
    out = o.reshape(B, L, D_MODEL).astype(h.dtype) @ P['gdn_w_o'][j]
    return out, new_buf.astype(conv_buf.dtype), s_new.astype(s0.dtype)


def swiglu(h, w_gate, w_up, w_down):
    return (jax.nn.silu(h @ w_gate) * (h @ w_up)) @ w_down


def moe_swiglu(h, w_router, b_router, w_gate, w_up, w_down):
    logits = (h @ w_router).astype(F32)
    _, top_i = lax.top_k(logits + b_router.astype(F32), TOP_K)
    probs = jax.nn.softmax(jnp.take_along_axis(logits, top_i, axis=-1), axis=-1)
    gates = jnp.sum(jax.nn.one_hot(top_i, N_EXPERTS, dtype=F32) * probs[..., None], axis=-2)
    out = jnp.zeros(h.shape, F32)
    for e in range(N_EXPERTS):
        out = out + gates[..., e:e + 1] * swiglu(h, w_gate[e], w_up[e], w_down[e])
    return out.astype(h.dtype)


def trunk(x, c, pos0, shift_st, wkv_st, ret_st, conv_st, gdn_st, P):
    new_shift, new_wkv, new_ret, new_conv, new_gdn = [], [], [], [], []
    v_first = None
    for i in range(DEPTH):
        j = i // N_MIXERS
        sh1, sc1, gt1, sh2, sc2, gt2 = adaln(c, P['w_ada'][i], P['b_ada'][i])
        h = rms_norm(x, P['norm1_g'][i]) * (1 + sc1) + sh1
        if i % N_MIXERS == 0:
            out, s_shift, s_wkv, v_first = rwkv7_time_mix(h, shift_st[j], wkv_st[j], v_first, P, j)
            new_shift.append(s_shift)
            new_wkv.append(s_wkv)
        elif i % N_MIXERS == 1:
            out, s_ret = retention(h, ret_st[j], pos0, P, j)
            new_ret.append(s_ret)
        else:
            out, s_conv, s_gdn = gated_deltanet(h, conv_st[j], gdn_st[j], P, j)
            new_conv.append(s_conv)
            new_gdn.append(s_gdn)
        x = x + gt1 * out
        h = rms_norm(x, P['norm2_g'][i]) * (1 + sc2) + sh2
        if i % 2 == 0:
            f = swiglu(h, P['ffn_w_gate'][i // 2], P['ffn_w_up'][i // 2], P['ffn_w_down'][i // 2])
        else:
            f = moe_swiglu(h, P['moe_w_router'][i // 2], P['moe_b_router'][i // 2], P['moe_w_gate'][i // 2],
                           P['moe_w_up'][i // 2], P['moe_w_down'][i // 2])
        x = x + gt2 * f
    y = rms_norm(x, P['final_g'])
    return (y, jnp.stack(new_shift), jnp.stack(new_wkv), jnp.stack(new_ret), jnp.stack(new_conv), jnp.stack(new_gdn))


def setup_inputs(seed: int = 0) -> dict:
    key = jax.random.key(seed)
    ks = iter(jax.random.split(key, 64))
    D = D_MODEL

    def nrm(shape, scale=1.0):
        return jax.random.normal(next(ks), shape, F32) * scale

    def unif(shape, lo, hi):
        return jax.random.uniform(next(ks), shape, F32, lo, hi)

    dt = jnp.exp(unif((N_C, C_HEADS), math.log(1e-3), math.log(1e-1)))
    return {
        'x_prompt': nrm((BATCH, SEQ, D)),
        'x_sample': nrm((DEC_BATCH, DEC_SEQ, D)),
        'c_prompt': nrm((BATCH, D)),
        'c_sample': nrm((DEC_BATCH, D)),
        'state_rwkv_shift': nrm((N_A, DEC_BATCH, D)),
        'state_rwkv_wkv': nrm((N_A, DEC_BATCH, A_HEADS, A_HEAD, A_HEAD), 0.1),
        'state_ret': nrm((N_B, DEC_BATCH, B_HEADS, B_QK, B_V), 0.5),
        'state_gdn_conv': nrm((N_C, DEC_BATCH, C_CONV - 1, 3 * D)),
        'state_gdn': nrm((N_C, DEC_BATCH, C_HEADS, C_HEAD, C_HEAD), 0.1),
        'norm1_g': 1.0 + nrm((DEPTH, D), 0.02),
        'norm2_g': 1.0 + nrm((DEPTH, D), 0.02),
        'w_ada': nrm((DEPTH, D, 6 * D), 0.5 * D ** -0.5),
        'b_ada': nrm((DEPTH, 6 * D), 0.02),
        'final_g': 1.0 + nrm((D,), 0.02),
        'rwkv_mu': unif((N_A, 6, D), 0.0, 1.0),
        'rwkv_w_rkv': nrm((N_A, 3, D, D), D ** -0.5),
        'rwkv_w0': unif((N_A, D), -6.0, 1.0),
        'rwkv_w1': nrm((N_A, D, A_DECAY_LORA), D ** -0.5),
        'rwkv_w2': nrm((N_A, A_DECAY_LORA, D), 0.1 * A_DECAY_LORA ** -0.5),
        'rwkv_a0': nrm((N_A, D), 0.1),
        'rwkv_a1': nrm((N_A, D, A_AAA_LORA), D ** -0.5),
        'rwkv_a2': nrm((N_A, A_AAA_LORA, D), 0.1 * A_AAA_LORA ** -0.5),
        'rwkv_v0': nrm((N_A - 1, D), 0.5),
        'rwkv_v1': nrm((N_A - 1, D, A_MV_LORA), D ** -0.5),
        'rwkv_v2': nrm((N_A - 1, A_MV_LORA, D), 0.1 * A_MV_LORA ** -0.5),
        'rwkv_g1': nrm((N_A, D, A_GATE_LORA), D ** -0.5),
        'rwkv_g2': nrm((N_A, A_GATE_LORA, D), A_GATE_LORA ** -0.5),
        'rwkv_k_k': 0.85 + nrm((N_A, D), 0.05),
        'rwkv_k_a': 1.0 + nrm((N_A, D), 0.05),
        'rwkv_r_k': nrm((N_A, A_HEADS, A_HEAD), 0.1),
        'rwkv_ln_w': 1.0 + nrm((N_A, D), 0.02),
        'rwkv_ln_b': nrm((N_A, D), 0.02),
        'rwkv_w_o': nrm((N_A, D, D), D ** -0.5),
        'ret_w_in': nrm((N_B, D, 6 * D), D ** -0.5),
        'ret_w_o': nrm((N_B, 2 * D, D), (2 * D) ** -0.5),
        'gdn_w_in': nrm((N_C, D, 4 * D + 2 * C_HEADS), D ** -0.5),
        'gdn_conv_w': nrm((N_C, C_CONV, 3 * D), C_CONV ** -0.5),
        'gdn_a_log': jnp.log(unif((N_C, C_HEADS), 1.0, 16.0)),
        'gdn_dt_bias': jnp.log(jnp.expm1(dt)),
        'gdn_norm_g': 1.0 + nrm((N_C, C_HEAD), 0.02),
        'gdn_w_o': nrm((N_C, D, D), D ** -0.5),
        'ffn_w_gate': nrm((N_DENSE, D, D_FF), D ** -0.5),
        'ffn_w_up': nrm((N_DENSE, D, D_FF), D ** -0.5),
        'ffn_w_down': nrm((N_DENSE, D_FF, D), D_FF ** -0.5),
        'moe_w_router': nrm((N_MOE, D, N_EXPERTS), D ** -0.5),
        'moe_b_router': nrm((N_MOE, N_EXPERTS), 0.01),
        'moe_w_gate': nrm((N_MOE, N_EXPERTS, D, D_FF_EXPERT), D ** -0.5),
        'moe_w_up': nrm((N_MOE, N_EXPERTS, D, D_FF_EXPERT), D ** -0.5),
        'moe_w_down': nrm((N_MOE, N_EXPERTS, D_FF_EXPERT, D), D_FF_EXPERT ** -0.5),
    }


def reference(x_prompt, x_sample, c_prompt, c_sample, state_rwkv_shift, state_rwkv_wkv, state_ret, state_gdn_conv,
              state_gdn, norm1_g, norm2_g, w_ada, b_ada, final_g, rwkv_mu, rwkv_w_rkv, rwkv_w0, rwkv_w1, rwkv_w2,
              rwkv_a0, rwkv_a1, rwkv_a2, rwkv_v0, rwkv_v1, rwkv_v2, rwkv_g1, rwkv_g2, rwkv_k_k, rwkv_k_a, rwkv_r_k,
              rwkv_ln_w, rwkv_ln_b, rwkv_w_o, ret_w_in, ret_w_o, gdn_w_in, gdn_conv_w, gdn_a_log, gdn_dt_bias,
              gdn_norm_g, gdn_w_o, ffn_w_gate, ffn_w_up, ffn_w_down, moe_w_router, moe_b_router, moe_w_gate,
              moe_w_up, moe_w_down):
    P = dict(norm1_g=norm1_g, norm2_g=norm2_g, w_ada=w_ada, b_ada=b_ada, final_g=final_g,
             rwkv_mu=rwkv_mu, rwkv_w_rkv=rwkv_w_rkv, rwkv_w0=rwkv_w0, rwkv_w1=rwkv_w1, rwkv_w2=rwkv_w2,
             rwkv_a0=rwkv_a0, rwkv_a1=rwkv_a1, rwkv_a2=rwkv_a2, rwkv_v0=rwkv_v0, rwkv_v1=rwkv_v1,
             rwkv_v2=rwkv_v2, rwkv_g1=rwkv_g1, rwkv_g2=rwkv_g2, rwkv_k_k=rwkv_k_k, rwkv_k_a=rwkv_k_a,
             rwkv_r_k=rwkv_r_k, rwkv_ln_w=rwkv_ln_w, rwkv_ln_b=rwkv_ln_b, rwkv_w_o=rwkv_w_o,
             ret_w_in=ret_w_in, ret_w_o=ret_w_o,
             gdn_w_in=gdn_w_in, gdn_conv_w=gdn_conv_w, gdn_a_log=gdn_a_log, gdn_dt_bias=gdn_dt_bias,
             gdn_norm_g=gdn_norm_g, gdn_w_o=gdn_w_o,
             ffn_w_gate=ffn_w_gate, ffn_w_up=ffn_w_up, ffn_w_down=ffn_w_down,
             moe_w_router=moe_w_router, moe_b_router=moe_b_router, moe_w_gate=moe_w_gate,
             moe_w_up=moe_w_up, moe_w_down=moe_w_down)
    bp = x_prompt.shape[0]
    zero_shift = jnp.zeros((N_A, bp, D_MODEL), state_rwkv_shift.dtype)
    zero_wkv = jnp.zeros((N_A, bp, A_HEADS, A_HEAD, A_HEAD), state_rwkv_wkv.dtype)
    zero_ret = jnp.zeros((N_B, bp, B_HEADS, B_QK, B_V), state_ret.dtype)
    zero_conv = jnp.zeros((N_C, bp, C_CONV - 1, 3 * D_MODEL), state_gdn_conv.dtype)
    zero_gdn = jnp.zeros((N_C, bp, C_HEADS, C_HEAD, C_HEAD), state_gdn.dtype)
    y_prompt, p_shift, p_wkv, p_ret, p_conv, p_gdn = trunk(
        x_prompt, c_prompt, 0, zero_shift, zero_wkv, zero_ret, zero_conv, zero_gdn, P)
    y_sample, s_shift, s_wkv, s_ret, s_conv, s_gdn = trunk(
        x_sample, c_sample, PAST_LEN, state_rwkv_shift, state_rwkv_wkv, state_ret, state_gdn_conv, state_gdn, P)
    return (y_prompt, y_sample, p_shift, s_shift, p_wkv, s_wkv, p_ret, s_ret, p_conv, s_conv, p_gdn, s_gdn)
```

```python
import functools
import math

import jax
import jax.numpy as jnp
from jax import lax
from jax.experimental import pallas as pl
from jax.experimental.pallas import tpu as pltpu

F32 = jnp.float32
BF16 = jnp.bfloat16

PAST_LEN = 16384
TOP_K = 2
NORM_EPS = 1e-6
A_GN_EPS = 64e-5
B_GN_EPS = 1e-6
ROPE_BASE = 10000.0

LANES = 128
SUBLANES = 8
VMEM_LIMIT = 56 * 2 ** 20
VMEM_BUDGET = 46 * 2 ** 20


def _divisor_tile(n, target, mult):
    best = None
    for t in range(mult, min(n, target) + 1, mult):
        if n % t == 0:
            best = t
    return best if best is not None else n


def _params(sem):
    return pltpu.CompilerParams(dimension_semantics=sem, vmem_limit_bytes=VMEM_LIMIT)


def _mm_kernel(a_ref, w_ref, o_ref, *scratch, nk, tk):
    if nk == 1:
        o_ref[...] = jnp.dot(a_ref[...].astype(BF16), w_ref[...].astype(BF16),
                             preferred_element_type=F32).astype(o_ref.dtype)
        return
    acc_ref, = scratch
    k = pl.program_id(2)
    a = a_ref[:, pl.ds(pl.multiple_of(k * tk, LANES), tk)]
    part = jnp.dot(a.astype(BF16), w_ref[...].astype(BF16), preferred_element_type=F32)

    @pl.when(k == 0)
    def _():
        acc_ref[...] = part

    @pl.when(k > 0)
    def _():
        acc_ref[...] += part

    @pl.when(k == nk - 1)
    def _():
        o_ref[...] = acc_ref[...].astype(o_ref.dtype)


def _mm_tiles(M, K, N, a_bytes, n_w):
    tk = K if K <= 4096 else _divisor_tile(K, 6144, LANES)
    for tm_t, tn_t in ((1664, 512), (1040, 512), (832, 512), (640, 512), (1040, 256), (640, 256), (416, 256),
                       (208, 256), (104, 128), (8, 128)):
        tm = _divisor_tile(M, tm_t, SUBLANES)
        if N % LANES == 0:
            tn = _divisor_tile(N, tn_t, LANES)
        else:
            tn = N if N <= tn_t else tn_t
        need = (2 * tm * K * a_bytes + n_w * (2 * tk * tn * 4 + tk * tn * 2) + tm * tk * 2 * (a_bytes == 4)
                + tm * tn * 4 * (2 + n_w + 1))
        if need <= VMEM_BUDGET:
            return tm, tn, tk
    return tm, tn, tk


def _w_spec(widx, tk, tn, imap):
    return pl.BlockSpec((None,) * len(widx) + (tk, tn), lambda *g: tuple(widx) + imap(*g))


def mm(a, w, widx=(), out_dtype=F32):
    M, K = a.shape
    N = w.shape[-1]
    tm, tn, tk = _mm_tiles(M, K, N, a.dtype.itemsize, 1)
    nk = K // tk
    grid = (M // tm, pl.cdiv(N, tn), nk)
    return pl.pallas_call(
        functools.partial(_mm_kernel, nk=nk, tk=tk),
        out_shape=jax.ShapeDtypeStruct((M, N), out_dtype),
        grid=grid,
        in_specs=[pl.BlockSpec((tm, K), lambda i, j, k: (i, 0)),
                  _w_spec(widx, tk, tn, lambda i, j, k: (k, j))],
        out_specs=pl.BlockSpec((tm, tn), lambda i, j, k: (i, j)),
        scratch_shapes=[pltpu.VMEM((tm, tn), F32)] if nk > 1 else [],
        compiler_params=_params(("parallel", "parallel", "arbitrary")),
        name="mm",
    )(a, w)


def _swiglu_up_kernel(a_ref, wg_ref, wu_ref, o_ref):
    a = a_ref[...].astype(BF16)
    g = jnp.dot(a, wg_ref[...].astype(BF16), preferred_element_type=F32)
    u = jnp.dot(a, wu_ref[...].astype(BF16), preferred_element_type=F32)
    o_ref[...] = (g * jax.nn.sigmoid(g) * u).astype(o_ref.dtype)


def swiglu_up(a, wg, wu, widx=()):
    M, K = a.shape
    F = wg.shape[-1]
    tm, tn, _ = _mm_tiles(M, K, F, a.dtype.itemsize, 2)
    return pl.pallas_call(
        _swiglu_up_kernel,
        out_shape=jax.ShapeDtypeStruct((M, F), BF16),
        grid=(M // tm, F // tn),
        in_specs=[pl.BlockSpec((tm, K), lambda i, j: (i, 0)),
                  _w_spec(widx, K, tn, lambda i, j: (0, j)),
                  _w_spec(widx, K, tn, lambda i, j: (0, j))],
        out_specs=pl.BlockSpec((tm, tn), lambda i, j: (i, j)),
        compiler_params=_params(("parallel", "parallel")),
        name="swiglu_up",
    )(a, wg, wu)


class _Rows:
    def __init__(self, b_p, L, b_s):
        self.b_p, self.L, self.b_s = b_p, L, b_s
        self.t_p = b_p * L
        self.T = self.t_p + b_s
        self.tr = min(LANES, math.gcd(L, b_s))
        assert self.tr % SUBLANES == 0
        self.n_pt = self.t_p // self.tr
        self.tiles_per_seq = L // self.tr
        self.n_tiles = self.T // self.tr

    def row_spec(self, D):
        return pl.BlockSpec((self.tr, D), lambda i: (i, 0))

    def mod_specs(self, D):
        n_pt, tps, b_p = self.n_pt, self.tiles_per_seq, self.b_p
        return [pl.BlockSpec((1, 1, D), lambda i: (jnp.minimum(i // tps, b_p - 1), 0, 0)),
                pl.BlockSpec((self.tr, D), lambda i: (jnp.maximum(i - n_pt, 0), 0))]

    def split_mod(self, m):
        return m[:self.b_p, None, :], m[self.b_p:]


def _resid_norm_kernel(*refs, n_pt, has_resid, has_mod, h_dtypes):
    it = iter(refs)
    x_ref = next(it)
    is_p = pl.program_id(0) < n_pt

    def mod():
        p_ref, s_ref = next(it), next(it)
        return jnp.where(is_p, p_ref[0], s_ref[...])

    x = x_ref[...]
    if has_resid:
        gate = mod()
        x = x + gate * next(it)[...]
    g_ref = next(it)
    if has_mod:
        sc, sh = mod(), mod()
    if has_resid:
        next(it)[...] = x
    h = x * lax.rsqrt(jnp.mean(jnp.square(x), axis=-1, keepdims=True) + NORM_EPS) * g_ref[...]
    if has_mod:
        h = h * (1 + sc) + sh
    for dt in h_dtypes:
        next(it)[...] = h.astype(dt)


def resid_norm(rows, x, g, y=None, gate=None, scale=None, shift=None, h_dtypes=(BF16,)):
    T, D = x.shape
    has_resid, has_mod = y is not None, scale is not None
    args, specs = [x], [rows.row_spec(D)]
    if has_resid:
        args += [*rows.split_mod(gate), y]
        specs += [*rows.mod_specs(D), rows.row_spec(D)]
    args.append(g.reshape(1, D))
    specs.append(pl.BlockSpec((1, D), lambda i: (0, 0)))
    if has_mod:
        args += [*rows.split_mod(scale), *rows.split_mod(shift)]
        specs += [*rows.mod_specs(D), *rows.mod_specs(D)]
    out_shape, out_specs = [], []
    if has_resid:
        out_shape.append(jax.ShapeDtypeStruct((T, D), F32))
        out_specs.append(rows.row_spec(D))
    for dt in h_dtypes:
        out_shape.append(jax.ShapeDtypeStruct((T, D), dt))
        out_specs.append(rows.row_spec(D))
    return pl.pallas_call(
        functools.partial(_resid_norm_kernel, n_pt=rows.n_pt, has_resid=has_resid, has_mod=has_mod,
                          h_dtypes=tuple(h_dtypes)),
        out_shape=out_shape, grid=(rows.n_tiles,), in_specs=specs, out_specs=out_specs,
        compiler_params=_params(("parallel",)),
        name="resid_norm",
    )(*args)


def _dplr_kernel(q_ref, w_ref, k_ref, a_ref, b_ref, v_ref, s0_ref, y_ref, s_ref, *, tb, dv, ib):
    @pl.when(pl.program_id(1) == 0)
    def _():
        s_ref[...] = s0_ref[...]

    def t_body(t, carry):
        def i_body(blk, carry):
            for ii in range(ib):
                i = blk * ib + ii
                s = s_ref[i]
                sa = jnp.sum(s * a_ref[t], axis=0, keepdims=True)
                sn = s * w_ref[t] + sa * b_ref[t] + v_ref[t, pl.ds(i, 1), :] * k_ref[t]
                s_ref[i] = sn
                y_ref[t, pl.ds(i, 1), :] = jnp.sum(sn * q_ref[t], axis=0, keepdims=True)
            return carry
        return lax.fori_loop(0, dv // ib, i_body, carry)

    lax.fori_loop(0, tb, t_body, 0)


def dplr_scan(q, w, k, a, b, v, s0):
    L, dk, NL = q.shape
    dv = v.shape[1]
    tb = _divisor_tile(L, 16, 1)
    ib = 4
    vec = lambda d: pl.BlockSpec((tb, d, LANES), lambda g, t: (t, 0, g))
    st = pl.BlockSpec((dv, dk, LANES), lambda g, t: (0, 0, g))
    return pl.pallas_call(
        functools.partial(_dplr_kernel, tb=tb, dv=dv, ib=ib),
        out_shape=[jax.ShapeDtypeStruct((L, dv, NL), F32), jax.ShapeDtypeStruct((dv, dk, NL), F32)],
        grid=(NL // LANES, L // tb),
        in_specs=[vec(dk)] * 5 + [vec(dv), st],
        out_specs=[vec(dv), st],
        compiler_params=_params(("parallel", "arbitrary")),
        name="dplr_scan",
    )(q, w, k, a, b, v, s0)


def _lanes_pad(n):
    return -n % LANES


def _to_lanes(t, B, L, H, d, pad):
    t = t.reshape(B, L, H, d).transpose(1, 3, 0, 2).reshape(L, d, B * H)
    return jnp.pad(t, ((0, 0), (0, 0), (0, pad))) if pad else t


def _from_lanes(t, B, L, H, d):
    return t[:, :, :B * H].reshape(L, d, B, H).transpose(2, 0, 3, 1).reshape(B * L, H * d)


def run_dplr(q, w, k, a, b, v, s0, B, L, H, dk, dv, state_is_kv):
    pad = _lanes_pad(B * H)
    ql, wl, kl, al, bl = (_to_lanes(t, B, L, H, dk, pad) for t in (q, w, k, a, b))
    vl = _to_lanes(v, B, L, H, dv, pad)
    s = s0.astype(F32).transpose((3, 2, 0, 1) if state_is_kv else (2, 3, 0, 1)).reshape(dv, dk, B * H)
    if pad:
        s = jnp.pad(s, ((0, 0), (0, 0), (0, pad)))
    y, s = dplr_scan(ql, wl, kl, al, bl, vl, s)
    s = s[:, :, :B * H].reshape(dv, dk, B, H).transpose((2, 3, 1, 0) if state_is_kv else (2, 3, 0, 1))
    return _from_lanes(y, B, L, H, dv), s


def _ret_kernel(lg_ref, cc_ref, cr_ref, q_ref, k_ref, v_ref, s0_ref, o_ref, s_ref, *, C):
    @pl.when(pl.program_id(2) == 0)
    def _():
        s_ref[...] = s0_ref[...]

    lg = lg_ref[0, 0:1, 0:1]
    gc = lg * cc_ref[...]
    gr = lg * cr_ref[...]
    row = lax.broadcasted_iota(jnp.int32, (C, C), 0)
    col = lax.broadcasted_iota(jnp.int32, (C, C), 1)
    causal = row >= col
    dec = jnp.where(causal, jnp.exp(jnp.where(causal, gc - gr, 0.0)), 0.0)
    q, k, v = q_ref[...], k_ref[...], v_ref[...]
    qb, kb, vb = q.astype(BF16), k.astype(BF16), v.astype(BF16)
    att = lax.dot_general(qb, kb, (((1,), (1,)), ((), ())), preferred_element_type=F32) * dec
    S = s_ref[0, 0]
    o = jnp.dot(att.astype(BF16), vb, preferred_element_type=F32)
    o = o + jnp.dot((q * jnp.exp(gc)).astype(BF16), S.astype(BF16), preferred_element_type=F32)
    o_ref[...] = o
    g_end = lg * cc_ref[C - 1:C, :]
    kd = (k * jnp.exp(g_end - gc)).astype(BF16)
    s_ref[0, 0] = S * jnp.exp(g_end) + lax.dot_general(kd, vb, (((0,), (0,)), ((), ())),
                                                       preferred_element_type=F32)


def retention_chunks(q, k, v, s0, lg, cnt, B, L, H, dk, dv, C):
    n = L // C
    lg_t = jnp.broadcast_to(lg.astype(F32)[:, None, None], (H, SUBLANES, LANES))
    tok = lambda d: pl.BlockSpec((C, d), lambda b, h, c: (b * n + c, h))
    st = pl.BlockSpec((1, 1, dk, dv), lambda b, h, c: (b, h, 0, 0))
    return pl.pallas_call(
        functools.partial(_ret_kernel, C=C),
        out_shape=[jax.ShapeDtypeStruct((B * L, H * dv), F32), jax.ShapeDtypeStruct((B, H, dk, dv), F32)],
        grid=(B, H, n),
        in_specs=[pl.BlockSpec((1, SUBLANES, LANES), lambda b, h, c: (h, 0, 0)),
                  pl.BlockSpec((C, 1), lambda b, h, c: (0, 0)),
                  pl.BlockSpec((1, C), lambda b, h, c: (0, 0)),
                  tok(dk), tok(dk), tok(dv), st],
        out_specs=[tok(dv), st],
        compiler_params=_params(("parallel", "parallel", "arbitrary")),
        name="retention",
    )(lg_t, cnt.reshape(C, 1), cnt.reshape(1, C), q, k, v, s0)


def _head_norm(y, eps):
    mu = jnp.mean(y, axis=-1, keepdims=True)
    var = jnp.mean(jnp.square(y - mu), axis=-1, keepdims=True)
    return (y - mu) * lax.rsqrt(var + eps)


def _rotary(x, pos, H, d):
    T = x.shape[0]
    inv = ROPE_BASE ** (-jnp.arange(0, d, 2, dtype=F32) / d)
    ang = pos.astype(F32)[:, None] * inv[None, :]
    cos, sin = jnp.cos(ang)[:, None, :], jnp.sin(ang)[:, None, :]
    x = x.reshape(T, H, d)
    x1, x2 = x[..., :d // 2], x[..., d // 2:]
    return jnp.concatenate([x1 * cos - x2 * sin, x1 * sin + x2 * cos], axis=-1).reshape(T, H * d)


def _rwkv(rows, h, shift_prev, s0, v_first, P, j):
    b_p, L, b_s, t_p = rows.b_p, rows.L, rows.b_s, rows.t_p
    D = h.shape[1]
    H, N = s0.shape[1], s0.shape[2]
    h_p = h[:t_p].reshape(b_p, L, D)
    prev = jnp.concatenate([jnp.concatenate([jnp.zeros((b_p, 1, D), F32), h_p[:, :-1]], axis=1).reshape(t_p, D),
                            shift_prev.astype(F32)], axis=0)
    new_shift_p, new_shift_s = h_p[:, -1], h[t_p:]
    mu = P['rwkv_mu'][j]
    xr, xk, xv, xw, xa, xg = ((h + (prev - h) * mu[s]).astype(BF16) for s in range(6))
    r = mm(xr, P['rwkv_w_rkv'], (j, 0))
    k = mm(xk, P['rwkv_w_rkv'], (j, 1))
    v = mm(xv, P['rwkv_w_rkv'], (j, 2))
    w_log = -jax.nn.softplus(-(P['rwkv_w0'][j] + mm(jnp.tanh(mm(xw, P['rwkv_w1'], (j,))), P['rwkv_w2'], (j,)))) - 0.5
    decay = jnp.exp(-jnp.exp(w_log))
    if v_first is None:
        v_first = v
    else:
        mix = jax.nn.sigmoid(P['rwkv_v0'][j - 1] + mm(mm(xv, P['rwkv_v1'], (j - 1,)), P['rwkv_v2'], (j - 1,)))
        v = v + (v_first - v) * mix
    a = jax.nn.sigmoid(P['rwkv_a0'][j] + mm(mm(xa, P['rwkv_a1'], (j,)), P['rwkv_a2'], (j,)))
    g = mm(jax.nn.sigmoid(mm(xg, P['rwkv_g1'], (j,))), P['rwkv_g2'], (j,))
    T = rows.T
    kk = (k * P['rwkv_k_k'][j]).reshape(T, H, N)
    kk = (kk / jnp.maximum(jnp.sqrt(jnp.sum(jnp.square(kk), axis=-1, keepdims=True)), 1e-12)).reshape(T, D)
    k = k * (1 + (a - 1) * P['rwkv_k_a'][j])
    na, nb = -kk, kk * a
    y_p, s_p = run_dplr(r[:t_p], decay[:t_p], k[:t_p], na[:t_p], nb[:t_p], v[:t_p],
                        jnp.zeros((b_p,) + s0.shape[1:], F32), b_p, L, H, N, N, False)
    y_s, s_s = run_dplr(r[t_p:], decay[t_p:], k[t_p:], na[t_p:], nb[t_p:], v[t_p:], s0, b_s, 1, H, N, N, False)
    y = jnp.concatenate([y_p, y_s], axis=0)
    y = _head_norm(y.reshape(T, H, N), A_GN_EPS).reshape(T, D) * P['rwkv_ln_w'][j] + P['rwkv_ln_b'][j]
    bonus = jnp.sum((r * k).reshape(T, H, N) * P['rwkv_r_k'][j], axis=-1, keepdims=True) * v.reshape(T, H, N)
    out = mm(((y + bonus.reshape(T, D)) * g).astype(BF16), P['rwkv_w_o'], (j,))
    return out, (new_shift_p, new_shift_s), (s_p, s_s), v_first


def _retention(rows, h, s0, P, j):
    b_p, L, b_s, t_p, T = rows.b_p, rows.L, rows.b_s, rows.t_p, rows.T
    D = h.shape[1]
    H, dk, dv = s0.shape[1], s0.shape[2], s0.shape[3]
    proj = mm(h, P['ret_w_in'], (j,))
    q, k, v, g = jnp.split(proj, [D, 2 * D, 4 * D], axis=-1)
    pos = jnp.concatenate([jnp.tile(jnp.arange(L), b_p), jnp.full((b_s,), PAST_LEN)])
    q = _rotary(q, pos, H, dk)
    k = _rotary(k, pos, H, dk) * dk ** -0.5
    lg = jnp.log1p(-jnp.exp2(-5.0 - jnp.arange(H, dtype=F32)))
    C = _divisor_tile(L, 256, SUBLANES)
    o_p, s_p = retention_chunks(q[:t_p], k[:t_p], v[:t_p], jnp.zeros((b_p,) + s0.shape[1:], F32), lg,
                                jnp.arange(1, C + 1, dtype=F32), b_p, L, H, dk, dv, C)
    pad8 = lambda t: jnp.pad(t[:, None, :], ((0, 0), (0, SUBLANES - 1), (0, 0))).reshape(b_s * SUBLANES, -1)
    o_s, s_s = retention_chunks(pad8(q[t_p:]), pad8(k[t_p:]), pad8(v[t_p:]), s0.astype(F32), lg,
                                jnp.ones((SUBLANES,), F32), b_s, SUBLANES, H, dk, dv, SUBLANES)
    o_s = o_s.reshape(b_s, SUBLANES, H * dv)[:, 0]
    o = jnp.concatenate([o_p, o_s], axis=0)
    o = _head_norm(o.reshape(T, H, dv), B_GN_EPS).reshape(T, H * dv)
    out = mm((jax.nn.silu(g) * o).astype(BF16), P['ret_w_o'], (j,))
    return out, (s_p, s_s)


def _gdn(rows, h, conv_buf, s0, P, j):
    b_p, L, b_s, t_p, T = rows.b_p, rows.L, rows.b_s, rows.t_p, rows.T
    D = h.shape[1]
    H, dh = s0.shape[1], s0.shape[2]
    n_conv = conv_buf.shape[1] + 1
    proj = mm(h, P['gdn_w_in'], (j,))
    qkv, z, a, b = jnp.split(proj, [3 * D, 4 * D, 4 * D + H], axis=-1)
    cat_p = jnp.concatenate([jnp.zeros((b_p, n_conv - 1, 3 * D), F32), qkv[:t_p].reshape(b_p, L, 3 * D)], axis=1)
    cat_s = jnp.concatenate([conv_buf.astype(F32), qkv[t_p:, None, :]], axis=1)
    new_buf = (cat_p[:, -(n_conv - 1):], cat_s[:, -(n_conv - 1):])
    cw = P['gdn_conv_w'][j]
    conv = lambda cat, n: sum(cat[:, i:i + n] * cw[i] for i in range(n_conv))
    qkv = jnp.concatenate([conv(cat_p, L).reshape(t_p, 3 * D), conv(cat_s, 1).reshape(b_s, 3 * D)], axis=0)
    qkv = jax.nn.silu(qkv)
    q, k, v = (t.reshape(T, H, dh) for t in jnp.split(qkv, 3, axis=-1))
    l2 = lambda x: x * lax.rsqrt(jnp.sum(jnp.square(x), axis=-1, keepdims=True) + 1e-6)
    q = l2(q) * dh ** -0.5
    k = l2(k)
    beta = jax.nn.sigmoid(b)[..., None]
    alpha = jnp.exp(-jnp.exp(P['gdn_a_log'][j].astype(F32)) * jax.nn.softplus(a + P['gdn_dt_bias'][j]))[..., None]
    flat = lambda t: jnp.broadcast_to(t, (T, H, dh)).reshape(T, D)
    ops = (flat(q), flat(alpha), flat(beta * k), flat(k), flat(-alpha * beta * k), flat(v))
    o_p, s_p = run_dplr(*(t[:t_p] for t in ops), jnp.zeros((b_p,) + s0.shape[1:], F32), b_p, L, H, dh, dh, True)
    o_s, s_s = run_dplr(*(t[t_p:] for t in ops), s0, b_s, 1, H, dh, dh, True)
    o = jnp.concatenate([o_p, o_s], axis=0).reshape(T, H, dh)
    o = o * lax.rsqrt(jnp.mean(jnp.square(o), axis=-1, keepdims=True) + NORM_EPS) * P['gdn_norm_g'][j]
    o = o * jax.nn.silu(z.reshape(T, H, dh))
    out = mm(o.reshape(T, D).astype(BF16), P['gdn_w_o'], (j,))
    return out, new_buf, (s_p, s_s)


def _moe(h, l, w_router, b_router, w_gate, w_up, w_down):
    n_e = w_router.shape[-1]
    logits = mm(h, w_router, (l,))
    _, top_i = lax.top_k(logits + b_router[l].astype(F32), TOP_K)
    probs = jax.nn.softmax(jnp.take_along_axis(logits, top_i, axis=-1), axis=-1)
    gates = jnp.sum(jax.nn.one_hot(top_i, n_e, dtype=F32) * probs[..., None], axis=-2)
    out = jnp.zeros(h.shape, F32)
    for e in range(n_e):
        out = out + gates[:, e:e + 1] * mm(swiglu_up(h, w_gate, w_up, (l, e)), w_down, (l, e))
    return out


def kernel(x_prompt, x_sample, c_prompt, c_sample, state_rwkv_shift, state_rwkv_wkv, state_ret, state_gdn_conv,
           state_gdn, norm1_g, norm2_g, w_ada, b_ada, final_g, rwkv_mu, rwkv_w_rkv, rwkv_w0, rwkv_w1, rwkv_w2,
           rwkv_a0, rwkv_a1, rwkv_a2, rwkv_v0, rwkv_v1, rwkv_v2, rwkv_g1, rwkv_g2, rwkv_k_k, rwkv_k_a, rwkv_r_k,
           rwkv_ln_w, rwkv_ln_b, rwkv_w_o, ret_w_in, ret_w_o, gdn_w_in, gdn_conv_w, gdn_a_log, gdn_dt_bias,
           gdn_norm_g, gdn_w_o, ffn_w_gate, ffn_w_up, ffn_w_down, moe_w_router, moe_b_router, moe_w_gate,
           moe_w_up, moe_w_down):
    P = dict(rwkv_mu=rwkv_mu, rwkv_w_rkv=rwkv_w_rkv, rwkv_w0=rwkv_w0, rwkv_w1=rwkv_w1, rwkv_w2=rwkv_w2,
             rwkv_a0=rwkv_a0, rwkv_a1=rwkv_a1, rwkv_a2=rwkv_a2, rwkv_v0=rwkv_v0, rwkv_v1=rwkv_v1,
             rwkv_v2=rwkv_v2, rwkv_g1=rwkv_g1, rwkv_g2=rwkv_g2, rwkv_k_k=rwkv_k_k, rwkv_k_a=rwkv_k_a,
             rwkv_r_k=rwkv_r_k, rwkv_ln_w=rwkv_ln_w, rwkv_ln_b=rwkv_ln_b, rwkv_w_o=rwkv_w_o,
             ret_w_in=ret_w_in, ret_w_o=ret_w_o, gdn_w_in=gdn_w_in, gdn_conv_w=gdn_conv_w, gdn_a_log=gdn_a_log,
             gdn_dt_bias=gdn_dt_bias, gdn_norm_g=gdn_norm_g, gdn_w_o=gdn_w_o)
    b_p, L, D = x_prompt.shape
    b_s = x_sample.shape[0]
    depth = norm1_g.shape[0]
    rows = _Rows(b_p, L, b_s)
    t_p = rows.t_p
    x = jnp.concatenate([x_prompt.reshape(t_p, D), x_sample.reshape(b_s, D)], axis=0)
    c_act = jax.nn.silu(jnp.concatenate([c_prompt, c_sample], axis=0))
    mods = [(mm(c_act, w_ada, (i,)) + b_ada[i]).reshape(b_p + b_s, 6, D) for i in range(depth)]
    is_rwkv = lambda i: i % 3 == 0

    hs = resid_norm(rows, x, norm1_g[0], scale=mods[0][:, 1], shift=mods[0][:, 0],
                    h_dtypes=(F32,) if is_rwkv(0) else (BF16,))
    h = hs[0]
    shifts, wkvs, rets, convs, gdns = [], [], [], [], []
    v_first = None
    for i in range(depth):
        j = i // 3
        m = mods[i]
        if i % 3 == 0:
            out, s_shift, s_wkv, v_first = _rwkv(rows, h, state_rwkv_shift[j], state_rwkv_wkv[j], v_first, P, j)
            shifts.append(s_shift)
            wkvs.append(s_wkv)
        elif i % 3 == 1:
            out, s_ret = _retention(rows, h, state_ret[j], P, j)
            rets.append(s_ret)
        else:
            out, s_conv, s_gdn = _gdn(rows, h, state_gdn_conv[j], state_gdn[j], P, j)
            convs.append(s_conv)
            gdns.append(s_gdn)
        x, h = resid_norm(rows, x, norm2_g[i], y=out, gate=m[:, 2], scale=m[:, 4], shift=m[:, 3])
        if i % 2 == 0:
            f = mm(swiglu_up(h, ffn_w_gate, ffn_w_up, (i // 2,)), ffn_w_down, (i // 2,))
        else:
            f = _moe(h, i // 2, moe_w_router, moe_b_router, moe_w_gate, moe_w_up, moe_w_down)
        if i + 1 < depth:
            mn = mods[i + 1]
            x, h = resid_norm(rows, x, norm1_g[i + 1], y=f, gate=m[:, 5], scale=mn[:, 1], shift=mn[:, 0],
                              h_dtypes=(F32,) if is_rwkv(i + 1) else (BF16,))
        else:
            x, h = resid_norm(rows, x, final_g, y=f, gate=m[:, 5], h_dtypes=(F32,))
    y = h
    stack = lambda pairs, k, dt: jnp.stack([p[k] for p in pairs]).astype(dt)
    return (y[:t_p].reshape(b_p, L, D), y[t_p:].reshape(b_s, 1, D),
            stack(shifts, 0, state_rwkv_shift.dtype), stack(shifts, 1, state_rwkv_shift.dtype),
            stack(wkvs, 0, state_rwkv_wkv.dtype), stack(wkvs, 1, state_rwkv_wkv.dtype),
            stack(rets, 0, state_ret.dtype), stack(rets, 1, state_ret.dtype),
            stack(convs, 0, state_gdn_conv.dtype), stack(convs, 1, state_gdn_conv.dtype),
            stack(gdns, 0, state_gdn.dtype), stack(gdns, 1, state_gdn.dtype))
```

```python
import functools
import math

import jax
import jax.numpy as jnp
from jax import lax
from jax.experimental import pallas as pl
from jax.experimental.pallas import tpu as pltpu

F32 = jnp.float32
BF16 = jnp.bfloat16

PAST_LEN = 16384
TOP_K = 2
NORM_EPS = 1e-6
A_GN_EPS = 64e-5
B_GN_EPS = 1e-6
ROPE_BASE = 10000.0

LANES = 128
SUBLANES = 8
VMEM_LIMIT = 56 * 2 ** 20
VMEM_BUDGET = 46 * 2 ** 20


def _divisor_tile(n, target, mult):
    best = None
    for t in range(mult, min(n, target) + 1, mult):
        if n % t == 0:
            best = t
    return best if best is not None else n


def _params(sem):
    return pltpu.CompilerParams(dimension_semantics=sem, vmem_limit_bytes=VMEM_LIMIT)


def _mm_kernel(a_ref, w_ref, o_ref, *scratch, nk, tk):
    if nk == 1:
        o_ref[...] = jnp.dot(a_ref[...].astype(BF16), w_ref[...].astype(BF16),
                             preferred_element_type=F32).astype(o_ref.dtype)
        return
    acc_ref, = scratch
    k = pl.program_id(2)
    a = a_ref[:, pl.ds(pl.multiple_of(k * tk, LANES), tk)]
    part = jnp.dot(a.astype(BF16), w_ref[...].astype(BF16), preferred_element_type=F32)

    @pl.when(k == 0)
    def _():
        acc_ref[...] = part

    @pl.when(k > 0)
    def _():
        acc_ref[...] += part

    @pl.when(k == nk - 1)
    def _():
        o_ref[...] = acc_ref[...].astype(o_ref.dtype)


def _mm_tiles(M, K, N, a_bytes, n_w):
    tk = K if K <= 4096 else _divisor_tile(K, 6144, LANES)
    for tm_t, tn_t in ((1664, 512), (1040, 512), (832, 512), (640, 512), (1040, 256), (640, 256), (416, 256),
                       (208, 256), (104, 128), (8, 128)):
        tm = _divisor_tile(M, tm_t, SUBLANES)
        if N % LANES == 0:
            tn = _divisor_tile(N, tn_t, LANES)
        else:
            tn = N if N <= tn_t else tn_t
        need = (2 * tm * K * a_bytes + n_w * (2 * tk * tn * 4 + tk * tn * 2) + tm * tk * 2 * (a_bytes == 4)
                + tm * tn * 4 * (2 + n_w + 1))
        if need <= VMEM_BUDGET:
            return tm, tn, tk
    return tm, tn, tk


def _w_spec(widx, tk, tn, imap):
    return pl.BlockSpec((None,) * len(widx) + (tk, tn), lambda *g: tuple(widx) + imap(*g))


def mm(a, w, widx=(), out_dtype=F32):
    M, K = a.shape
    N = w.shape[-1]
    tm, tn, tk = _mm_tiles(M, K, N, a.dtype.itemsize, 1)
    nk = K // tk
    grid = (M // tm, pl.cdiv(N, tn), nk)
    return pl.pallas_call(
        functools.partial(_mm_kernel, nk=nk, tk=tk),
        out_shape=jax.ShapeDtypeStruct((M, N), out_dtype),
        grid=grid,
        in_specs=[pl.BlockSpec((tm, K), lambda i, j, k: (i, 0)),
                  _w_spec(widx, tk, tn, lambda i, j, k: (k, j))],
        out_specs=pl.BlockSpec((tm, tn), lambda i, j, k: (i, j)),
        scratch_shapes=[pltpu.VMEM((tm, tn), F32)] if nk > 1 else [],
        compiler_params=_params(("parallel", "parallel", "arbitrary")),
        name="mm",
    )(a, w)


def _swiglu_up_kernel(a_ref, wg_ref, wu_ref, o_ref):
    a = a_ref[...].astype(BF16)
    g = jnp.dot(a, wg_ref[...].astype(BF16), preferred_element_type=F32)
    u = jnp.dot(a, wu_ref[...].astype(BF16), preferred_element_type=F32)
    o_ref[...] = (g * jax.nn.sigmoid(g) * u).astype(o_ref.dtype)


def swiglu_up(a, wg, wu, widx=()):
    M, K = a.shape
    F = wg.shape[-1]
    tm, tn, _ = _mm_tiles(M, K, F, a.dtype.itemsize, 2)
    return pl.pallas_call(
        _swiglu_up_kernel,
        out_shape=jax.ShapeDtypeStruct((M, F), BF16),
        grid=(M // tm, F // tn),
        in_specs=[pl.BlockSpec((tm, K), lambda i, j: (i, 0)),
                  _w_spec(widx, K, tn, lambda i, j: (0, j)),
                  _w_spec(widx, K, tn, lambda i, j: (0, j))],
        out_specs=pl.BlockSpec((tm, tn), lambda i, j: (i, j)),
        compiler_params=_params(("parallel", "parallel")),
        name="swiglu_up",
    )(a, wg, wu)


class _Rows:
    def __init__(self, b_p, L, b_s):
        self.b_p, self.L, self.b_s = b_p, L, b_s
        self.t_p = b_p * L
        self.T = self.t_p + b_s
        self.tr = min(LANES, math.gcd(L, b_s))
        assert self.tr % SUBLANES == 0
        self.n_pt = self.t_p // self.tr
        self.tiles_per_seq = L // self.tr
        self.n_tiles = self.T // self.tr

    def row_spec(self, D):
        return pl.BlockSpec((self.tr, D), lambda i: (i, 0))

    def mod_specs(self, D):
        n_pt, tps, b_p = self.n_pt, self.tiles_per_seq, self.b_p
        return [pl.BlockSpec((1, 1, D), lambda i: (jnp.minimum(i // tps, b_p - 1), 0, 0)),
                pl.BlockSpec((self.tr, D), lambda i: (jnp.maximum(i - n_pt, 0), 0))]

    def split_mod(self, m):
        return m[:self.b_p, None, :], m[self.b_p:]


def _resid_norm_kernel(*refs, n_pt, has_resid, has_mod, h_dtypes):
    it = iter(refs)
    x_ref = next(it)
    is_p = pl.program_id(0) < n_pt

    def mod():
        p_ref, s_ref = next(it), next(it)
        return jnp.where(is_p, p_ref[0], s_ref[...])

    x = x_ref[...]
    if has_resid == 1:
        gate = mod()
        x = x + gate * next(it)[...]
    elif has_resid == 2:
        gate = mod()
        y1_ref, y2_ref, p1_ref, p2_ref = next(it), next(it), next(it), next(it)
        x = x + gate * (p1_ref[...] * y1_ref[...] + p2_ref[...] * y2_ref[...])
    g_ref = next(it)
    if has_mod:
        sc, sh = mod(), mod()
    if has_resid:
        next(it)[...] = x
    h = x * lax.rsqrt(jnp.mean(jnp.square(x), axis=-1, keepdims=True) + NORM_EPS) * g_ref[...]
    if has_mod:
        h = h * (1 + sc) + sh
    for dt in h_dtypes:
        next(it)[...] = h.astype(dt)


def resid_norm(rows, x, g, y=None, gate=None, scale=None, shift=None, h_dtypes=(BF16,), pair=None):
    T, D = x.shape
    has_resid, has_mod = 2 if pair is not None else int(y is not None), scale is not None
    args, specs = [x], [rows.row_spec(D)]
    if has_resid == 1:
        args += [*rows.split_mod(gate), y]
        specs += [*rows.mod_specs(D), rows.row_spec(D)]
    elif has_resid == 2:
        y2, p1, p2 = pair
        n_t, tr = rows.n_tiles, rows.tr
        col = pl.BlockSpec((tr, 1), lambda i: (i, 0))
        args += [*rows.split_mod(gate), y2, y2, p1, p2]
        specs += [*rows.mod_specs(D), rows.row_spec(D), pl.BlockSpec((tr, D), lambda i: (i + n_t, 0)), col, col]
    args.append(g.reshape(1, D))
    specs.append(pl.BlockSpec((1, D), lambda i: (0, 0)))
    if has_mod:
        args += [*rows.split_mod(scale), *rows.split_mod(shift)]
        specs += [*rows.mod_specs(D), *rows.mod_specs(D)]
    out_shape, out_specs = [], []
    if has_resid:
        out_shape.append(jax.ShapeDtypeStruct((T, D), F32))
        out_specs.append(rows.row_spec(D))
    for dt in h_dtypes:
        out_shape.append(jax.ShapeDtypeStruct((T, D), dt))
        out_specs.append(rows.row_spec(D))
    return pl.pallas_call(
        functools.partial(_resid_norm_kernel, n_pt=rows.n_pt, has_resid=has_resid, has_mod=has_mod,
                          h_dtypes=tuple(h_dtypes)),
        out_shape=out_shape, grid=(rows.n_tiles,), in_specs=specs, out_specs=out_specs,
        compiler_params=_params(("parallel",)),
        name="resid_norm",
    )(*args)


def _dplr_kernel(q_ref, w_ref, k_ref, a_ref, b_ref, v_ref, s0_ref, y_ref, s_ref, *, tb, dv, ib):
    @pl.when(pl.program_id(1) == 0)
    def _():
        s_ref[...] = s0_ref[...]

    def t_body(t, carry):
        def i_body(blk, carry):
            for ii in range(ib):
                i = blk * ib + ii
                s = s_ref[i]
                sa = jnp.sum(s * a_ref[t], axis=0, keepdims=True)
                sn = s * w_ref[t] + sa * b_ref[t] + v_ref[t, pl.ds(i, 1), :] * k_ref[t]
                s_ref[i] = sn
                y_ref[t, pl.ds(i, 1), :] = jnp.sum(sn * q_ref[t], axis=0, keepdims=True)
            return carry
        return lax.fori_loop(0, dv // ib, i_body, carry)

    lax.fori_loop(0, tb, t_body, 0)


def dplr_scan(q, w, k, a, b, v, s0):
    L, dk, NL = q.shape
    dv = v.shape[1]
    tb = _divisor_tile(L, 16, 1)
    ib = 4
    vec = lambda d: pl.BlockSpec((tb, d, LANES), lambda g, t: (t, 0, g))
    st = pl.BlockSpec((dv, dk, LANES), lambda g, t: (0, 0, g))
    return pl.pallas_call(
        functools.partial(_dplr_kernel, tb=tb, dv=dv, ib=ib),
        out_shape=[jax.ShapeDtypeStruct((L, dv, NL), F32), jax.ShapeDtypeStruct((dv, dk, NL), F32)],
        grid=(NL // LANES, L // tb),
        in_specs=[vec(dk)] * 5 + [vec(dv), st],
        out_specs=[vec(dv), st],
        compiler_params=_params(("parallel", "arbitrary")),
        name="dplr_scan",
    )(q, w, k, a, b, v, s0)


def _lanes_pad(n):
    return -n % LANES


def _to_lanes(t, B, L, H, d, pad):
    t = t.reshape(B, L, H, d).transpose(1, 3, 0, 2).reshape(L, d, B * H)
    return jnp.pad(t, ((0, 0), (0, 0), (0, pad))) if pad else t


def _from_lanes(t, B, L, H, d):
    return t[:, :, :B * H].reshape(L, d, B, H).transpose(2, 0, 3, 1).reshape(B * L, H * d)


def run_dplr(q, w, k, a, b, v, s0, B, L, H, dk, dv, state_is_kv):
    pad = _lanes_pad(B * H)
    ql, wl, kl, al, bl = (_to_lanes(t, B, L, H, dk, pad) for t in (q, w, k, a, b))
    vl = _to_lanes(v, B, L, H, dv, pad)
    s = s0.astype(F32).transpose((3, 2, 0, 1) if state_is_kv else (2, 3, 0, 1)).reshape(dv, dk, B * H)
    if pad:
        s = jnp.pad(s, ((0, 0), (0, 0), (0, pad)))
    y, s = dplr_scan(ql, wl, kl, al, bl, vl, s)
    s = s[:, :, :B * H].reshape(dv, dk, B, H).transpose((2, 3, 1, 0) if state_is_kv else (2, 3, 0, 1))
    return _from_lanes(y, B, L, H, dv), s


def _ret_kernel(lg_ref, cc_ref, cr_ref, q_ref, k_ref, v_ref, s0_ref, o_ref, s_ref, *, C):
    @pl.when(pl.program_id(2) == 0)
    def _():
        s_ref[...] = s0_ref[...]

    lg = lg_ref[0, 0:1, 0:1]
    gc = lg * cc_ref[...]
    gr = lg * cr_ref[...]
    row = lax.broadcasted_iota(jnp.int32, (C, C), 0)
    col = lax.broadcasted_iota(jnp.int32, (C, C), 1)
    causal = row >= col
    dec = jnp.where(causal, jnp.exp(jnp.where(causal, gc - gr, 0.0)), 0.0)
    q, k, v = q_ref[...], k_ref[...], v_ref[...]
    qb, kb, vb = q.astype(BF16), k.astype(BF16), v.astype(BF16)
    att = lax.dot_general(qb, kb, (((1,), (1,)), ((), ())), preferred_element_type=F32) * dec
    S = s_ref[0, 0]
    o = jnp.dot(att.astype(BF16), vb, preferred_element_type=F32)
    o = o + jnp.dot((q * jnp.exp(gc)).astype(BF16), S.astype(BF16), preferred_element_type=F32)
    o_ref[...] = o
    g_end = lg * cc_ref[C - 1:C, :]
    kd = (k * jnp.exp(g_end - gc)).astype(BF16)
    s_ref[0, 0] = S * jnp.exp(g_end) + lax.dot_general(kd, vb, (((0,), (0,)), ((), ())),
                                                       preferred_element_type=F32)


def retention_chunks(q, k, v, s0, lg, cnt, B, L, H, dk, dv, C):
    n = L // C
    lg_t = jnp.broadcast_to(lg.astype(F32)[:, None, None], (H, SUBLANES, LANES))
    tok = lambda d: pl.BlockSpec((C, d), lambda b, h, c: (b * n + c, h))
    st = pl.BlockSpec((1, 1, dk, dv), lambda b, h, c: (b, h, 0, 0))
    return pl.pallas_call(
        functools.partial(_ret_kernel, C=C),
        out_shape=[jax.ShapeDtypeStruct((B * L, H * dv), F32), jax.ShapeDtypeStruct((B, H, dk, dv), F32)],
        grid=(B, H, n),
        in_specs=[pl.BlockSpec((1, SUBLANES, LANES), lambda b, h, c: (h, 0, 0)),
                  pl.BlockSpec((C, 1), lambda b, h, c: (0, 0)),
                  pl.BlockSpec((1, C), lambda b, h, c: (0, 0)),
                  tok(dk), tok(dk), tok(dv), st],
        out_specs=[tok(dv), st],
        compiler_params=_params(("parallel", "parallel", "arbitrary")),
        name="retention",
    )(lg_t, cnt.reshape(C, 1), cnt.reshape(1, C), q, k, v, s0)


def _bdot(a, b, dims=((1,), (0,))):
    return lax.dot_general(a.astype(BF16), b.astype(BF16), (dims, ((), ())), preferred_element_type=F32)


def _unit_lower_inverses(ms, C):
    row = lax.broadcasted_iota(jnp.int32, (C, C), 0)
    col = lax.broadcasted_iota(jnp.int32, (C, C), 1)
    eye = jnp.where(row == col, 1.0, 0.0)
    ts = [eye - jnp.where((row >> 1) == (col >> 1), m, 0.0) for m in ms]
    sh = 1
    while (1 << sh) < C:
        same_big = (row >> (sh + 1)) == (col >> (sh + 1))
        same_small = (row >> sh) == (col >> sh)
        tl = [_bdot(t, jnp.where(same_big, jnp.where(same_small, 0.0, m), 0.0)) for t, m in zip(ts, ms)]
        ts = [t - _bdot(x, t) for t, x in zip(ts, tl)]
        sh += 1
    return ts


def _gdn_kernel(q_ref, k_ref, v_ref, gc_ref, gr_ref, b_ref, s0_ref, o_ref, s_ref, *, C, nc):
    @pl.when(pl.program_id(2) == 0)
    def _():
        s_ref[...] = s0_ref[...]

    row = lax.broadcasted_iota(jnp.int32, (C, C), 0)
    col = lax.broadcasted_iota(jnp.int32, (C, C), 1)
    cs = range(nc)
    rows = [slice(c * C, (c + 1) * C) for c in cs]
    q = [q_ref[r, :] for r in rows]
    k = [k_ref[r, :] for r in rows]
    gc = [gc_ref[0, 0, r, :] for r in rows]
    beta = [b_ref[0, 0, r, :] for r in rows]
    dec = [jnp.exp(jnp.where(row >= col, gc[c] - gr_ref[0, 0, c], -jnp.inf)) for c in cs]
    kb = [k[c] * beta[c] for c in cs]
    ms = [jnp.where(row > col, _bdot(kb[c], k[c], ((1,), (1,))) * dec[c], 0.0) for c in cs]
    attn = [_bdot(q[c], k[c], ((1,), (1,))) * dec[c] for c in cs]
    ts = _unit_lower_inverses(ms, C)
    u = [_bdot(ts[c], v_ref[rows[c], :] * beta[c]) for c in cs]
    w = [_bdot(ts[c], kb[c] * jnp.exp(gc[c])) for c in cs]
    g_end = [gc[c][C - 1:C, :] for c in cs]
    kd = [k[c] * jnp.exp(g_end[c] - gc[c]) for c in cs]
    lhs = [jnp.concatenate([q[c] * jnp.exp(gc[c]) - _bdot(attn[c], w[c]), _bdot(kd[c], w[c], ((0,), (0,)))],
                           axis=0).astype(BF16) for c in cs]
    o_u = [_bdot(attn[c], u[c]) for c in cs]
    s_u = [_bdot(kd[c], u[c], ((0,), (0,))) for c in cs]
    S = s_ref[0, 0]
    for c in cs:
        ls = _bdot(lhs[c], S)
        o_ref[rows[c], :] = ls[:C] + o_u[c]
        S = S * jnp.exp(g_end[c]) - ls[C:] + s_u[c]
    s_ref[0, 0] = S


def gdn_chunks(q, k, v, g, beta, s0, B, L, H, d):
    C = _divisor_tile(L, 64, SUBLANES)
    n = L // C
    nc = _divisor_tile(n, 8, 1)
    G = jnp.cumsum(g.reshape(B, n, C, H), axis=2).transpose(0, 3, 1, 2)
    tok = pl.BlockSpec((nc * C, d), lambda b, h, c: (b * (n // nc) + c, h))
    colv = pl.BlockSpec((1, 1, nc * C, 1), lambda b, h, c: (b, h, c, 0))
    st = pl.BlockSpec((1, 1, d, d), lambda b, h, c: (b, h, 0, 0))
    return pl.pallas_call(
        functools.partial(_gdn_kernel, C=C, nc=nc),
        out_shape=[jax.ShapeDtypeStruct((B * L, H * d), F32), jax.ShapeDtypeStruct((B, H, d, d), F32)],
        grid=(B, H, n // nc),
        in_specs=[tok, tok, tok, colv, pl.BlockSpec((1, 1, nc, 1, C), lambda b, h, c: (b, h, c, 0, 0)), colv, st],
        out_specs=[tok, st],
        compiler_params=_params(("parallel", "parallel", "arbitrary")),
        name="gdn_chunks",
    )(q, k, v, G.reshape(B, H, L, 1), G.reshape(B, H, n, 1, C),
      beta.reshape(B, L, H).transpose(0, 2, 1).reshape(B, H, L, 1), s0)


def _rwkv_kernel(r_ref, k_ref, v_ref, a_ref, b_ref, g_ref, s0_ref, y_ref, s_ref, *, C, nc, N):
    @pl.when(pl.program_id(2) == 0)
    def _():
        s_ref[...] = s0_ref[...]

    n_h = r_ref.shape[1] // N
    row = lax.broadcasted_iota(jnp.int32, (C, C), 0)
    col = lax.broadcasted_iota(jnp.int32, (C, C), 1)
    strict, incl = row > col, row >= col
    ones_lower = jnp.where(incl, 1.0, 0.0).astype(BF16)
    rows = [slice(c * C, (c + 1) * C) for c in range(nc)]
    at, rt, bh, kh, bt, kt, e_end = [], [], [], [], [], [], []
    for rs in rows:
        g = g_ref[rs, :]
        g_hi = g.astype(BF16)
        rem = g - g_hi.astype(F32)
        g_mid = rem.astype(BF16)
        g_lo = (rem - g_mid.astype(F32)).astype(BF16)
        G = sum(jnp.dot(ones_lower, p, preferred_element_type=F32) for p in (g_hi, g_mid, g_lo))
        e_neg = jnp.exp(-G)
        to_end = jnp.exp(G[C - 1:C, :] - G)
        b, k = b_ref[rs, :], k_ref[rs, :]
        at.append(a_ref[rs, :] * jnp.exp(G - g))
        rt.append(r_ref[rs, :] * jnp.exp(G))
        bh.append(b * e_neg)
        kh.append(k * e_neg)
        bt.append(b * to_end)
        kt.append(k * to_end)
        e_end.append(jnp.exp(G[C - 1:C, :]))
    chains = [(c, h) for c in range(nc) for h in range(n_h)]
    hl = lambda x, h: x[:, h * N:(h + 1) * N]
    nt = ((1,), (1,))
    tn = ((0,), (0,))
    a_ab = [jnp.where(strict, _bdot(hl(at[c], h), hl(bh[c], h), nt), 0.0) for c, h in chains]
    a_ak = [jnp.where(strict, _bdot(hl(at[c], h), hl(kh[c], h), nt), 0.0) for c, h in chains]
    a_rb = [jnp.where(incl, _bdot(hl(rt[c], h), hl(bh[c], h), nt), 0.0) for c, h in chains]
    a_rk = [jnp.where(incl, _bdot(hl(rt[c], h), hl(kh[c], h), nt), 0.0) for c, h in chains]
    v = [hl(v_ref[rows[c], :], h) for c, h in chains]
    ts = _unit_lower_inverses([-m for m in a_ab], C)
    t_a = [_bdot(ts[i], hl(at[c], h)) for i, (c, h) in enumerate(chains)]
    akv = [_bdot(a_ak[i], v[i]) for i in range(len(chains))]
    uv = [_bdot(ts[i], akv[i]) for i in range(len(chains))]
    lhs_y = [hl(rt[c], h) + _bdot(a_rb[i], t_a[i]) for i, (c, h) in enumerate(chains)]
    m2 = [_bdot(t_a[i], hl(bt[c], h), tn) for i, (c, h) in enumerate(chains)]
    y_c = [_bdot(a_rb[i], uv[i]) + _bdot(a_rk[i], v[i]) for i in range(len(chains))]
    s_c = [_bdot(uv[i], hl(bt[c], h), tn) + _bdot(v[i], hl(kt[c], h), tn) for i, (c, h) in enumerate(chains)]
    S = [s_ref[0, h] for h in range(n_h)]
    for i, (c, h) in enumerate(chains):
        y_ref[rows[c], h * N:(h + 1) * N] = _bdot(lhs_y[i], S[h], nt) + y_c[i]
        S[h] = S[h] * hl(e_end[c], h) + _bdot(S[h], m2[i]) + s_c[i]
    for h in range(n_h):
        s_ref[0, h] = S[h]


def rwkv_chunks(r, k, v, a, b, g, s0, B, L, H, N):
    C = _divisor_tile(L, 64, SUBLANES)
    n = L // C
    nc = _divisor_tile(n, 4, 1)
    n_h = LANES // N
    tok = pl.BlockSpec((nc * C, LANES), lambda b_, p, c: (b_ * (n // nc) + c, p))
    st = pl.BlockSpec((1, n_h, N, N), lambda b_, p, c: (b_, p, 0, 0))
    return pl.pallas_call(
        functools.partial(_rwkv_kernel, C=C, nc=nc, N=N),
        out_shape=[jax.ShapeDtypeStruct((B * L, H * N), F32), jax.ShapeDtypeStruct((B, H, N, N), F32)],
        grid=(B, H // n_h, n // nc),
        in_specs=[tok] * 6 + [st],
        out_specs=[tok, st],
        compiler_params=_params(("parallel", "parallel", "arbitrary")),
        name="rwkv_chunks",
    )(r, k, v, a, b, g, s0)


def _head_norm(y, eps):
    mu = jnp.mean(y, axis=-1, keepdims=True)
    var = jnp.mean(jnp.square(y - mu), axis=-1, keepdims=True)
    return (y - mu) * lax.rsqrt(var + eps)


def _rotary(x, pos, H, d):
    T = x.shape[0]
    inv = ROPE_BASE ** (-jnp.arange(0, d, 2, dtype=F32) / d)
    ang = pos.astype(F32)[:, None] * inv[None, :]
    cos, sin = jnp.cos(ang)[:, None, :], jnp.sin(ang)[:, None, :]
    x = x.reshape(T, H, d)
    x1, x2 = x[..., :d // 2], x[..., d // 2:]
    return jnp.concatenate([x1 * cos - x2 * sin, x1 * sin + x2 * cos], axis=-1).reshape(T, H * d)


def _rwkv(rows, h, shift_prev, s0, v_first, P, j):
    b_p, L, b_s, t_p = rows.b_p, rows.L, rows.b_s, rows.t_p
    D = h.shape[1]
    H, N = s0.shape[1], s0.shape[2]
    h_p = h[:t_p].reshape(b_p, L, D)
    prev = jnp.concatenate([jnp.concatenate([jnp.zeros((b_p, 1, D), F32), h_p[:, :-1]], axis=1).reshape(t_p, D),
                            shift_prev.astype(F32)], axis=0)
    new_shift_p, new_shift_s = h_p[:, -1], h[t_p:]
    mu = P['rwkv_mu'][j]
    xr, xk, xv, xw, xa, xg = ((h + (prev - h) * mu[s]).astype(BF16) for s in range(6))
    r = mm(xr, P['rwkv_w_rkv'], (j, 0))
    k = mm(xk, P['rwkv_w_rkv'], (j, 1))
    v = mm(xv, P['rwkv_w_rkv'], (j, 2))
    w_log = -jax.nn.softplus(-(P['rwkv_w0'][j] + mm(jnp.tanh(mm(xw, P['rwkv_w1'], (j,))), P['rwkv_w2'], (j,)))) - 0.5
    log_w = -jnp.exp(w_log)
    if v_first is None:
        v_first = v
    else:
        mix = jax.nn.sigmoid(P['rwkv_v0'][j - 1] + mm(mm(xv, P['rwkv_v1'], (j - 1,)), P['rwkv_v2'], (j - 1,)))
        v = v + (v_first - v) * mix
    a = jax.nn.sigmoid(P['rwkv_a0'][j] + mm(mm(xa, P['rwkv_a1'], (j,)), P['rwkv_a2'], (j,)))
    g = mm(jax.nn.sigmoid(mm(xg, P['rwkv_g1'], (j,))), P['rwkv_g2'], (j,))
    T = rows.T
    kk = (k * P['rwkv_k_k'][j]).reshape(T, H, N)
    kk = (kk / jnp.maximum(jnp.sqrt(jnp.sum(jnp.square(kk), axis=-1, keepdims=True)), 1e-12)).reshape(T, D)
    k = k * (1 + (a - 1) * P['rwkv_k_a'][j])
    na, nb = -kk, kk * a
    y_p, s_p = rwkv_chunks(r, k, v, na, nb, log_w, jnp.zeros((b_p,) + s0.shape[1:], F32), b_p, L, H, N)
    y_s, s_s = run_dplr(r[t_p:], jnp.exp(log_w[t_p:]), k[t_p:], na[t_p:], nb[t_p:], v[t_p:], s0,
                        b_s, 1, H, N, N, False)
    y = jnp.concatenate([y_p, y_s], axis=0)
    y = _head_norm(y.reshape(T, H, N), A_GN_EPS).reshape(T, D) * P['rwkv_ln_w'][j] + P['rwkv_ln_b'][j]
    bonus = jnp.sum((r * k).reshape(T, H, N) * P['rwkv_r_k'][j], axis=-1, keepdims=True) * v.reshape(T, H, N)
    out = mm(((y + bonus.reshape(T, D)) * g).astype(BF16), P['rwkv_w_o'], (j,))
    return out, (new_shift_p, new_shift_s), (s_p, s_s), v_first


def _retention(rows, h, s0, P, j):
    b_p, L, b_s, t_p, T = rows.b_p, rows.L, rows.b_s, rows.t_p, rows.T
    D = h.shape[1]
    H, dk, dv = s0.shape[1], s0.shape[2], s0.shape[3]
    proj = mm(h, P['ret_w_in'], (j,))
    q, k, v, g = jnp.split(proj, [D, 2 * D, 4 * D], axis=-1)
    pos = jnp.concatenate([jnp.tile(jnp.arange(L), b_p), jnp.full((b_s,), PAST_LEN)])
    q = _rotary(q, pos, H, dk)
    k = _rotary(k, pos, H, dk) * dk ** -0.5
    lg = jnp.log1p(-jnp.exp2(-5.0 - jnp.arange(H, dtype=F32)))
    C = _divisor_tile(L, 256, SUBLANES)
    o_p, s_p = retention_chunks(q[:t_p], k[:t_p], v[:t_p], jnp.zeros((b_p,) + s0.shape[1:], F32), lg,
                                jnp.arange(1, C + 1, dtype=F32), b_p, L, H, dk, dv, C)
    pad8 = lambda t: jnp.pad(t[:, None, :], ((0, 0), (0, SUBLANES - 1), (0, 0))).reshape(b_s * SUBLANES, -1)
    o_s, s_s = retention_chunks(pad8(q[t_p:]), pad8(k[t_p:]), pad8(v[t_p:]), s0.astype(F32), lg,
                                jnp.ones((SUBLANES,), F32), b_s, SUBLANES, H, dk, dv, SUBLANES)
    o_s = o_s.reshape(b_s, SUBLANES, H * dv)[:, 0]
    o = jnp.concatenate([o_p, o_s], axis=0)
    o = _head_norm(o.reshape(T, H, dv), B_GN_EPS).reshape(T, H * dv)
    out = mm((jax.nn.silu(g) * o).astype(BF16), P['ret_w_o'], (j,))
    return out, (s_p, s_s)


def _gdn(rows, h, conv_buf, s0, P, j):
    b_p, L, b_s, t_p, T = rows.b_p, rows.L, rows.b_s, rows.t_p, rows.T
    D = h.shape[1]
    H, dh = s0.shape[1], s0.shape[2]
    n_conv = conv_buf.shape[1] + 1
    proj = mm(h, P['gdn_w_in'], (j,))
    qkv, z, a, b = jnp.split(proj, [3 * D, 4 * D, 4 * D + H], axis=-1)
    cat_p = jnp.concatenate([jnp.zeros((b_p, n_conv - 1, 3 * D), F32), qkv[:t_p].reshape(b_p, L, 3 * D)], axis=1)
    cat_s = jnp.concatenate([conv_buf.astype(F32), qkv[t_p:, None, :]], axis=1)
    new_buf = (cat_p[:, -(n_conv - 1):], cat_s[:, -(n_conv - 1):])
    cw = P['gdn_conv_w'][j]
    conv = lambda cat, n: sum(cat[:, i:i + n] * cw[i] for i in range(n_conv))
    qkv = jnp.concatenate([conv(cat_p, L).reshape(t_p, 3 * D), conv(cat_s, 1).reshape(b_s, 3 * D)], axis=0)
    qkv = jax.nn.silu(qkv)
    q, k, v = (t.reshape(T, H, dh) for t in jnp.split(qkv, 3, axis=-1))
    l2 = lambda x: x * lax.rsqrt(jnp.sum(jnp.square(x), axis=-1, keepdims=True) + 1e-6)
    q = l2(q) * dh ** -0.5
    k = l2(k)
    beta = jax.nn.sigmoid(b)
    log_alpha = -jnp.exp(P['gdn_a_log'][j].astype(F32)) * jax.nn.softplus(a + P['gdn_dt_bias'][j])
    o_p, s_p = gdn_chunks(q.reshape(T, D), k.reshape(T, D), v.reshape(T, D), log_alpha[:t_p], beta[:t_p],
                          jnp.zeros((b_p,) + s0.shape[1:], F32), b_p, L, H, dh)
    qs, ks, vs = q[t_p:], k[t_p:], v[t_p:]
    al, be = jnp.exp(log_alpha[t_p:])[..., None], beta[t_p:, :, None]
    flat = lambda t: jnp.broadcast_to(t, (b_s, H, dh)).reshape(b_s, D)
    o_s, s_s = run_dplr(flat(qs), flat(al), flat(be * ks), flat(ks), flat(-al * be * ks), flat(vs), s0,
                        b_s, 1, H, dh, dh, True)
    o = jnp.concatenate([o_p[:t_p], o_s], axis=0).reshape(T, H, dh)
    o = o * lax.rsqrt(jnp.mean(jnp.square(o), axis=-1, keepdims=True) + NORM_EPS) * P['gdn_norm_g'][j]
    o = o * jax.nn.silu(z.reshape(T, H, dh))
    out = mm(o.reshape(T, D).astype(BF16), P['gdn_w_o'], (j,))
    return out, new_buf, (s_p, s_s)


def _router_kernel(l_ref, b_ref, i1_ref, i2_ref, p1_ref, p2_ref, *, n_e):
    logit = l_ref[...]
    s = logit + b_ref[...]
    lane = lax.broadcasted_iota(jnp.int32, s.shape, 1)
    first = lambda v: jnp.min(jnp.where(v == jnp.max(v, axis=-1, keepdims=True), lane, n_e), axis=-1, keepdims=True)
    i1 = first(s)
    i2 = first(jnp.where(lane == i1, -jnp.inf, s))
    pick = lambda i: jnp.sum(jnp.where(lane == i, logit, 0.0), axis=-1, keepdims=True)
    l1, l2 = pick(i1), pick(i2)
    m = jnp.maximum(l1, l2)
    e1, e2 = jnp.exp(l1 - m), jnp.exp(l2 - m)
    i1_ref[...] = i1
    i2_ref[...] = i2
    p1_ref[...] = e1 / (e1 + e2)
    p2_ref[...] = e2 / (e1 + e2)


def router_top2(logits, bias):
    T, n_e = logits.shape
    tr = _divisor_tile(T, 1040, SUBLANES)
    col = lambda: pl.BlockSpec((tr, 1), lambda i: (i, 0))
    return pl.pallas_call(
        functools.partial(_router_kernel, n_e=n_e),
        out_shape=[jax.ShapeDtypeStruct((T, 1), jnp.int32)] * 2 + [jax.ShapeDtypeStruct((T, 1), F32)] * 2,
        grid=(T // tr,),
        in_specs=[pl.BlockSpec((tr, n_e), lambda i: (i, 0)), pl.BlockSpec((1, n_e), lambda i: (0, 0))],
        out_specs=[col(), col(), col(), col()],
        compiler_params=_params(("parallel",)),
        name="router_top2",
    )(logits, bias.reshape(1, n_e).astype(F32))


def _gather_kernel(idx_ref, src_ref, o_ref, sem, *, tg):
    base = pl.program_id(0) * tg
    row_copy = lambda src_row, r: pltpu.make_async_copy(src_ref.at[pl.ds(src_row, 1)], o_ref.at[pl.ds(r, 1)], sem)

    def start(r, c):
        row_copy(idx_ref[base + r], r).start()
        return c

    def wait(r, c):
        row_copy(0, r).wait()
        return c

    lax.fori_loop(0, tg, start, 0)
    lax.fori_loop(0, tg, wait, 0)


def gather_rows(src, idx, tg):
    n_out = idx.shape[0]
    D = src.shape[1]
    return pl.pallas_call(
        functools.partial(_gather_kernel, tg=tg),
        out_shape=jax.ShapeDtypeStruct((n_out, D), src.dtype),
        grid_spec=pltpu.PrefetchScalarGridSpec(
            num_scalar_prefetch=1, grid=(n_out // tg,),
            in_specs=[pl.BlockSpec(memory_space=pl.ANY)],
            out_specs=pl.BlockSpec((tg, D), lambda i, idx: (i, 0)),
            scratch_shapes=[pltpu.SemaphoreType.DMA(())]),
        compiler_params=_params(("arbitrary",)),
        name="gather_rows",
    )(idx, src)


def _group_up_kernel(te_ref, ts_ref, ok_ref, a_ref, wg_ref, wu_ref, o_ref):
    ok = ok_ref[pl.program_id(0)] > 0

    @pl.when(ok)
    def _():
        _swiglu_up_kernel(a_ref, wg_ref, wu_ref, o_ref)

    @pl.when(jnp.logical_not(ok))
    def _():
        o_ref[...] = jnp.zeros_like(o_ref)


def _group_down_kernel(te_ref, ts_ref, ok_ref, a_ref, w_ref, o_ref, *scratch, nk, tk):
    ok = ok_ref[pl.program_id(0)] > 0

    @pl.when(ok)
    def _():
        _mm_kernel(a_ref, w_ref, o_ref, *scratch, nk=nk, tk=tk)

    @pl.when(jnp.logical_not(ok))
    def _():
        o_ref[...] = jnp.zeros_like(o_ref)


def grouped_swiglu_up(a, wg, wu, l, tiles, tm):
    M, K = a.shape
    F = wg.shape[-1]
    _, tn, _ = _mm_tiles(tm, K, F, a.dtype.itemsize, 2)
    nj = F // tn
    w_spec = lambda: pl.BlockSpec((None, None, K, tn),
                                  lambda t, j, te, ts, ok: (l, te[t], 0, jnp.where(ok[t] > 0, j, nj - 1)))
    return pl.pallas_call(
        _group_up_kernel,
        out_shape=jax.ShapeDtypeStruct((M, F), BF16),
        grid_spec=pltpu.PrefetchScalarGridSpec(
            num_scalar_prefetch=3, grid=(M // tm, nj),
            in_specs=[pl.BlockSpec((tm, K), lambda t, j, te, ts, ok: (ts[t], 0)), w_spec(), w_spec()],
            out_specs=pl.BlockSpec((tm, tn), lambda t, j, te, ts, ok: (t, j))),
        compiler_params=_params(("arbitrary", "arbitrary")),
        name="grouped_swiglu_up",
    )(*tiles, a, wg, wu)


def grouped_mm(a, w, l, tiles, tm):
    M, K = a.shape
    N = w.shape[-1]
    _, tn, tk = _mm_tiles(tm, K, N, a.dtype.itemsize, 1)
    nk, nj = K // tk, N // tn

    def w_map(t, j, k, te, ts, ok):
        live = ok[t] > 0
        return l, te[t], jnp.where(live, k, nk - 1), jnp.where(live, j, nj - 1)

    return pl.pallas_call(
        functools.partial(_group_down_kernel, nk=nk, tk=tk),
        out_shape=jax.ShapeDtypeStruct((M, N), F32),
        grid_spec=pltpu.PrefetchScalarGridSpec(
            num_scalar_prefetch=3, grid=(M // tm, nj, nk),
            in_specs=[pl.BlockSpec((tm, K), lambda t, j, k, te, ts, ok: (ts[t], 0)),
                      pl.BlockSpec((None, None, tk, tn), w_map)],
            out_specs=pl.BlockSpec((tm, tn), lambda t, j, k, te, ts, ok: (t, j)),
            scratch_shapes=[pltpu.VMEM((tm, tn), F32)] if nk > 1 else []),
        compiler_params=_params(("arbitrary", "arbitrary", "arbitrary")),
        name="grouped_mm",
    )(*tiles, a, w)


def _moe(rows, h, l, w_router, b_router, w_gate, w_up, w_down):
    T, D = h.shape
    n_e = w_router.shape[-1]
    i1, i2, p1, p2 = router_top2(mm(h, w_router, (l,)), b_router[l])
    tm = 512 if TOP_K * T >= 8 * 512 else 64
    n_tiles = pl.cdiv(TOP_K * T, tm) + n_e
    pair_e = jnp.concatenate([i1[:, 0], i2[:, 0]])
    onehot = (pair_e[:, None] == jnp.arange(n_e)[None, :]).astype(jnp.int32)
    rank = jnp.take_along_axis(jnp.cumsum(onehot, axis=0), pair_e[:, None], axis=1)[:, 0] - 1
    count = jnp.sum(onehot, axis=0)
    padded = (count + tm - 1) // tm * tm
    g_end = jnp.cumsum(padded)
    slot = ((g_end - padded)[pair_e] + rank).astype(jnp.int32)
    pair_tok = jnp.concatenate([jnp.arange(T, dtype=jnp.int32)] * TOP_K)
    src = jnp.zeros((n_tiles * tm,), jnp.int32).at[slot].set(pair_tok)
    n_used = g_end[-1] // tm
    t_id = jnp.minimum(jnp.arange(n_tiles), n_used - 1).astype(jnp.int32)
    tile_e = jnp.minimum(jnp.sum(t_id[:, None] * tm >= g_end[None, :], axis=1), n_e - 1).astype(jnp.int32)
    tile_ok = (jnp.arange(n_tiles) < n_used).astype(jnp.int32)
    tiles = (tile_e, t_id, tile_ok)
    hg = gather_rows(h, src, min(tm, 256))
    yg = grouped_mm(grouped_swiglu_up(hg, w_gate, w_up, l, tiles, tm), w_down, l, tiles, tm)
    return gather_rows(yg, slot, rows.tr), p1, p2


def kernel(x_prompt, x_sample, c_prompt, c_sample, state_rwkv_shift, state_rwkv_wkv, state_ret, state_gdn_conv,
           state_gdn, norm1_g, norm2_g, w_ada, b_ada, final_g, rwkv_mu, rwkv_w_rkv, rwkv_w0, rwkv_w1, rwkv_w2,
           rwkv_a0, rwkv_a1, rwkv_a2, rwkv_v0, rwkv_v1, rwkv_v2, rwkv_g1, rwkv_g2, rwkv_k_k, rwkv_k_a, rwkv_r_k,
           rwkv_ln_w, rwkv_ln_b, rwkv_w_o, ret_w_in, ret_w_o, gdn_w_in, gdn_conv_w, gdn_a_log, gdn_dt_bias,
           gdn_norm_g, gdn_w_o, ffn_w_gate, ffn_w_up, ffn_w_down, moe_w_router, moe_b_router, moe_w_gate,
           moe_w_up, moe_w_down):
    P = dict(rwkv_mu=rwkv_mu, rwkv_w_rkv=rwkv_w_rkv, rwkv_w0=rwkv_w0, rwkv_w1=rwkv_w1, rwkv_w2=rwkv_w2,
             rwkv_a0=rwkv_a0, rwkv_a1=rwkv_a1, rwkv_a2=rwkv_a2, rwkv_v0=rwkv_v0, rwkv_v1=rwkv_v1,
             rwkv_v2=rwkv_v2, rwkv_g1=rwkv_g1, rwkv_g2=rwkv_g2, rwkv_k_k=rwkv_k_k, rwkv_k_a=rwkv_k_a,
             rwkv_r_k=rwkv_r_k, rwkv_ln_w=rwkv_ln_w, rwkv_ln_b=rwkv_ln_b, rwkv_w_o=rwkv_w_o,
             ret_w_in=ret_w_in, ret_w_o=ret_w_o, gdn_w_in=gdn_w_in, gdn_conv_w=gdn_conv_w, gdn_a_log=gdn_a_log,
             gdn_dt_bias=gdn_dt_bias, gdn_norm_g=gdn_norm_g, gdn_w_o=gdn_w_o)
    b_p, L, D = x_prompt.shape
    b_s = x_sample.shape[0]
    depth = norm1_g.shape[0]
    rows = _Rows(b_p, L, b_s)
    t_p = rows.t_p
    x = jnp.concatenate([x_prompt.reshape(t_p, D), x_sample.reshape(b_s, D)], axis=0)
    c_act = jax.nn.silu(jnp.concatenate([c_prompt, c_sample], axis=0))
    mods = [(mm(c_act, w_ada, (i,)) + b_ada[i]).reshape(b_p + b_s, 6, D) for i in range(depth)]
    is_rwkv = lambda i: i % 3 == 0

    hs = resid_norm(rows, x, norm1_g[0], scale=mods[0][:, 1], shift=mods[0][:, 0],
                    h_dtypes=(F32,) if is_rwkv(0) else (BF16,))
    h = hs[0]
    shifts, wkvs, rets, convs, gdns = [], [], [], [], []
    v_first = None
    for i in range(depth):
        j = i // 3
        m = mods[i]
        if i % 3 == 0:
            out, s_shift, s_wkv, v_first = _rwkv(rows, h, state_rwkv_shift[j], state_rwkv_wkv[j], v_first, P, j)
            shifts.append(s_shift)
            wkvs.append(s_wkv)
        elif i % 3 == 1:
            out, s_ret = _retention(rows, h, state_ret[j], P, j)
            rets.append(s_ret)
        else:
            out, s_conv, s_gdn = _gdn(rows, h, state_gdn_conv[j], state_gdn[j], P, j)
            convs.append(s_conv)
            gdns.append(s_gdn)
        x, h = resid_norm(rows, x, norm2_g[i], y=out, gate=m[:, 2], scale=m[:, 4], shift=m[:, 3],
                          h_dtypes=(BF16,) if i % 2 == 0 else (F32,))
        if i % 2 == 0:
            f = dict(y=mm(swiglu_up(h, ffn_w_gate, ffn_w_up, (i // 2,)), ffn_w_down, (i // 2,)))
        else:
            f = dict(pair=_moe(rows, h, i // 2, moe_w_router, moe_b_router, moe_w_gate, moe_w_up, moe_w_down))
        if i + 1 < depth:
            mn = mods[i + 1]
            x, h = resid_norm(rows, x, norm1_g[i + 1], gate=m[:, 5], scale=mn[:, 1], shift=mn[:, 0],
                              h_dtypes=(F32,) if is_rwkv(i + 1) else (BF16,), **f)
        else:
            x, h = resid_norm(rows, x, final_g, gate=m[:, 5], h_dtypes=(F32,), **f)
    y = h
    stack = lambda pairs, k, dt: jnp.stack([p[k] for p in pairs]).astype(dt)
    return (y[:t_p].reshape(b_p, L, D), y[t_p:].reshape(b_s, 1, D),
            stack(shifts, 0, state_rwkv_shift.dtype), stack(shifts, 1, state_rwkv_shift.dtype),
            stack(wkvs, 0, state_rwkv_wkv.dtype), stack(wkvs, 1, state_rwkv_wkv.dtype),
            stack(rets, 0, state_ret.dtype), stack(rets, 1, state_ret.dtype),
            stack(convs, 0, state_gdn_conv.dtype), stack(convs, 1, state_gdn_conv.dtype),
            stack(gdns, 0, state_gdn.dtype), stack(gdns, 1, state_gdn.dtype))
```

```python
import functools
import math

import jax
import jax.numpy as jnp
from jax import lax
from jax.experimental import pallas as pl
from jax.experimental.pallas import tpu as pltpu

F32 = jnp.float32
BF16 = jnp.bfloat16

PAST_LEN = 16384
TOP_K = 2
NORM_EPS = 1e-6
A_GN_EPS = 64e-5
B_GN_EPS = 1e-6
ROPE_BASE = 10000.0

LANES = 128
SUBLANES = 8
VMEM_LIMIT = 56 * 2 ** 20
VMEM_BUDGET = 46 * 2 ** 20


def _divisor_tile(n, target, mult):
    best = None
    for t in range(mult, min(n, target) + 1, mult):
        if n % t == 0:
            best = t
    return best if best is not None else n


def _params(sem):
    return pltpu.CompilerParams(dimension_semantics=sem, vmem_limit_bytes=VMEM_LIMIT)


def _mm_kernel(a_ref, w_ref, o_ref, *scratch, nk, tk):
    if nk == 1:
        o_ref[...] = jnp.dot(a_ref[...].astype(BF16), w_ref[...].astype(BF16),
                             preferred_element_type=F32).astype(o_ref.dtype)
        return
    acc_ref, = scratch
    k = pl.program_id(2)
    a = a_ref[:, pl.ds(pl.multiple_of(k * tk, LANES), tk)]
    part = jnp.dot(a.astype(BF16), w_ref[...].astype(BF16), preferred_element_type=F32)

    @pl.when(k == 0)
    def _():
        acc_ref[...] = part

    @pl.when(k > 0)
    def _():
        acc_ref[...] += part

    @pl.when(k == nk - 1)
    def _():
        o_ref[...] = acc_ref[...].astype(o_ref.dtype)


def _mm_tiles(M, K, N, a_bytes, n_w, tm_fixed=None):
    tk = K if K <= 4096 else _divisor_tile(K, 6144, LANES)
    for tm_t, tn_t in ((1664, 512), (1040, 512), (832, 512), (640, 512), (1040, 256), (640, 256), (416, 256),
                       (208, 256), (104, 128), (8, 128)):
        tm = tm_fixed or _divisor_tile(M, tm_t, SUBLANES)
        if N % LANES == 0:
            tn = _divisor_tile(N, tn_t, LANES)
        else:
            tn = N if N <= tn_t else tn_t
        need = (2 * tm * K * a_bytes + n_w * (2 * tk * tn * 4 + tk * tn * 2) + tm * tk * 2 * (a_bytes == 4)
                + tm * tn * 4 * (2 + n_w + 1))
        if need <= VMEM_BUDGET:
            return tm, tn, tk
    return tm, tn, tk


def _w_spec(widx, tk, tn, imap):
    return pl.BlockSpec((None,) * len(widx) + (tk, tn), lambda *g: tuple(widx) + imap(*g))


def mm(a, w, widx=(), out_dtype=F32):
    M, K = a.shape
    N = w.shape[-1]
    tm, tn, tk = _mm_tiles(M, K, N, a.dtype.itemsize, 1)
    nk = K // tk
    grid = (M // tm, pl.cdiv(N, tn), nk)
    return pl.pallas_call(
        functools.partial(_mm_kernel, nk=nk, tk=tk),
        out_shape=jax.ShapeDtypeStruct((M, N), out_dtype),
        grid=grid,
        in_specs=[pl.BlockSpec((tm, K), lambda i, j, k: (i, 0)),
                  _w_spec(widx, tk, tn, lambda i, j, k: (k, j))],
        out_specs=pl.BlockSpec((tm, tn), lambda i, j, k: (i, j)),
        scratch_shapes=[pltpu.VMEM((tm, tn), F32)] if nk > 1 else [],
        compiler_params=_params(("parallel", "parallel", "arbitrary")),
        name="mm",
    )(a, w)


def _swiglu_up_kernel(a_ref, wg_ref, wu_ref, o_ref):
    a = a_ref[...].astype(BF16)
    g = jnp.dot(a, wg_ref[...].astype(BF16), preferred_element_type=F32)
    u = jnp.dot(a, wu_ref[...].astype(BF16), preferred_element_type=F32)
    o_ref[...] = (g * jax.nn.sigmoid(g) * u).astype(o_ref.dtype)


def swiglu_up(a, wg, wu, widx=()):
    M, K = a.shape
    F = wg.shape[-1]
    tm, tn, _ = _mm_tiles(M, K, F, a.dtype.itemsize, 2)
    return pl.pallas_call(
        _swiglu_up_kernel,
        out_shape=jax.ShapeDtypeStruct((M, F), BF16),
        grid=(M // tm, F // tn),
        in_specs=[pl.BlockSpec((tm, K), lambda i, j: (i, 0)),
                  _w_spec(widx, K, tn, lambda i, j: (0, j)),
                  _w_spec(widx, K, tn, lambda i, j: (0, j))],
        out_specs=pl.BlockSpec((tm, tn), lambda i, j: (i, j)),
        compiler_params=_params(("parallel", "parallel")),
        name="swiglu_up",
    )(a, wg, wu)


class _Rows:
    def __init__(self, b_p, L, b_s):
        self.b_p, self.L, self.b_s = b_p, L, b_s
        self.t_p = b_p * L
        self.T = self.t_p + b_s
        self.tr = min(LANES, math.gcd(L, b_s))
        assert self.tr % SUBLANES == 0
        self.n_pt = self.t_p // self.tr
        self.tiles_per_seq = L // self.tr
        self.n_tiles = self.T // self.tr

    def row_spec(self, D):
        return pl.BlockSpec((self.tr, D), lambda i: (i, 0))

    def mod_specs(self, D):
        n_pt, tps, b_p = self.n_pt, self.tiles_per_seq, self.b_p
        return [pl.BlockSpec((1, 1, D), lambda i: (jnp.minimum(i // tps, b_p - 1), 0, 0)),
                pl.BlockSpec((self.tr, D), lambda i: (jnp.maximum(i - n_pt, 0), 0))]

    def split_mod(self, m):
        return m[:self.b_p, None, :], m[self.b_p:]


def _resid_norm_kernel(*refs, n_pt, has_resid, has_mod, h_dtypes):
    it = iter(refs)
    x_ref = next(it)
    is_p = pl.program_id(0) < n_pt

    def mod():
        p_ref, s_ref = next(it), next(it)
        return jnp.where(is_p, p_ref[0], s_ref[...])

    x = x_ref[...]
    if has_resid == 1:
        gate = mod()
        x = x + gate * next(it)[...]
    elif has_resid == 2:
        gate = mod()
        y1_ref, y2_ref, p1_ref, p2_ref = next(it), next(it), next(it), next(it)
        x = x + gate * (p1_ref[...] * y1_ref[...] + p2_ref[...] * y2_ref[...])
    g_ref = next(it)
    if has_mod:
        sc, sh = mod(), mod()
    if has_resid:
        next(it)[...] = x
    h = x * lax.rsqrt(jnp.mean(jnp.square(x), axis=-1, keepdims=True) + NORM_EPS) * g_ref[...]
    if has_mod:
        h = h * (1 + sc) + sh
    for dt in h_dtypes:
        next(it)[...] = h.astype(dt)


def resid_norm(rows, x, g, y=None, gate=None, scale=None, shift=None, h_dtypes=(BF16,), pair=None):
    T, D = x.shape
    has_resid, has_mod = 2 if pair is not None else int(y is not None), scale is not None
    args, specs = [x], [rows.row_spec(D)]
    if has_resid == 1:
        args += [*rows.split_mod(gate), y]
        specs += [*rows.mod_specs(D), rows.row_spec(D)]
    elif has_resid == 2:
        y2, p1, p2 = pair
        n_t, tr = rows.n_tiles, rows.tr
        col = pl.BlockSpec((tr, 1), lambda i: (i, 0))
        args += [*rows.split_mod(gate), y2, y2, p1, p2]
        specs += [*rows.mod_specs(D), rows.row_spec(D), pl.BlockSpec((tr, D), lambda i: (i + n_t, 0)), col, col]
    args.append(g.reshape(1, D))
    specs.append(pl.BlockSpec((1, D), lambda i: (0, 0)))
    if has_mod:
        args += [*rows.split_mod(scale), *rows.split_mod(shift)]
        specs += [*rows.mod_specs(D), *rows.mod_specs(D)]
    out_shape, out_specs = [], []
    if has_resid:
        out_shape.append(jax.ShapeDtypeStruct((T, D), F32))
        out_specs.append(rows.row_spec(D))
    for dt in h_dtypes:
        out_shape.append(jax.ShapeDtypeStruct((T, D), dt))
        out_specs.append(rows.row_spec(D))
    return pl.pallas_call(
        functools.partial(_resid_norm_kernel, n_pt=rows.n_pt, has_resid=has_resid, has_mod=has_mod,
                          h_dtypes=tuple(h_dtypes)),
        out_shape=out_shape, grid=(rows.n_tiles,), in_specs=specs, out_specs=out_specs,
        compiler_params=_params(("parallel",)),
        name="resid_norm",
    )(*args)


def _dplr_kernel(q_ref, w_ref, k_ref, a_ref, b_ref, v_ref, s0_ref, y_ref, s_ref, *, tb, dv, ib):
    @pl.when(pl.program_id(1) == 0)
    def _():
        s_ref[...] = s0_ref[...]

    def t_body(t, carry):
        def i_body(blk, carry):
            for ii in range(ib):
                i = blk * ib + ii
                s = s_ref[i]
                sa = jnp.sum(s * a_ref[t], axis=0, keepdims=True)
                sn = s * w_ref[t] + sa * b_ref[t] + v_ref[t, pl.ds(i, 1), :] * k_ref[t]
                s_ref[i] = sn
                y_ref[t, pl.ds(i, 1), :] = jnp.sum(sn * q_ref[t], axis=0, keepdims=True)
            return carry
        return lax.fori_loop(0, dv // ib, i_body, carry)

    lax.fori_loop(0, tb, t_body, 0)


def dplr_scan(q, w, k, a, b, v, s0):
    L, dk, NL = q.shape
    dv = v.shape[1]
    tb = _divisor_tile(L, 16, 1)
    ib = 4
    vec = lambda d: pl.BlockSpec((tb, d, LANES), lambda g, t: (t, 0, g))
    st = pl.BlockSpec((dv, dk, LANES), lambda g, t: (0, 0, g))
    return pl.pallas_call(
        functools.partial(_dplr_kernel, tb=tb, dv=dv, ib=ib),
        out_shape=[jax.ShapeDtypeStruct((L, dv, NL), F32), jax.ShapeDtypeStruct((dv, dk, NL), F32)],
        grid=(NL // LANES, L // tb),
        in_specs=[vec(dk)] * 5 + [vec(dv), st],
        out_specs=[vec(dv), st],
        compiler_params=_params(("parallel", "arbitrary")),
        name="dplr_scan",
    )(q, w, k, a, b, v, s0)


def _lanes_pad(n):
    return -n % LANES


def _to_lanes(t, B, L, H, d, pad):
    t = t.reshape(B, L, H, d).transpose(1, 3, 0, 2).reshape(L, d, B * H)
    return jnp.pad(t, ((0, 0), (0, 0), (0, pad))) if pad else t


def _from_lanes(t, B, L, H, d):
    return t[:, :, :B * H].reshape(L, d, B, H).transpose(2, 0, 3, 1).reshape(B * L, H * d)


def run_dplr(q, w, k, a, b, v, s0, B, L, H, dk, dv, state_is_kv):
    pad = _lanes_pad(B * H)
    ql, wl, kl, al, bl = (_to_lanes(t, B, L, H, dk, pad) for t in (q, w, k, a, b))
    vl = _to_lanes(v, B, L, H, dv, pad)
    s = s0.astype(F32).transpose((3, 2, 0, 1) if state_is_kv else (2, 3, 0, 1)).reshape(dv, dk, B * H)
    if pad:
        s = jnp.pad(s, ((0, 0), (0, 0), (0, pad)))
    y, s = dplr_scan(ql, wl, kl, al, bl, vl, s)
    s = s[:, :, :B * H].reshape(dv, dk, B, H).transpose((2, 3, 1, 0) if state_is_kv else (2, 3, 0, 1))
    return _from_lanes(y, B, L, H, dv), s


def _ret_kernel(lg_ref, cc_ref, cr_ref, cos_ref, sin_ref, q_ref, k_ref, v_ref, g_ref, s0_ref, o_ref, s_ref,
                *, C, hb, dk, dv):
    @pl.when(pl.program_id(2) == 0)
    def _():
        s_ref[...] = s0_ref[...]

    row = lax.broadcasted_iota(jnp.int32, (C, C), 0)
    col = lax.broadcasted_iota(jnp.int32, (C, C), 1)
    causal = row >= col
    cos, sin = cos_ref[...], sin_ref[...]
    half = dk // 2

    def rotary(x):
        x1, x2 = x[:, :half], x[:, half:]
        return jnp.concatenate([x1 * cos - x2 * sin, x1 * sin + x2 * cos], axis=-1)

    for h in range(hb):
        lg = lg_ref[h, 0:1, 0:1]
        gc = lg * cc_ref[...]
        gr = lg * cr_ref[...]
        dec = jnp.where(causal, jnp.exp(jnp.where(causal, gc - gr, 0.0)), 0.0)
        q = rotary(q_ref[:, h * dk:(h + 1) * dk])
        k = rotary(k_ref[:, h * dk:(h + 1) * dk]) * dk ** -0.5
        v = v_ref[:, h * dv:(h + 1) * dv]
        S = s_ref[0, h]
        o = _bdot(_bdot(q, k, ((1,), (1,))) * dec, v) + _bdot(q * jnp.exp(gc), S)
        g_end = lg * cc_ref[C - 1:C, :]
        s_ref[0, h] = S * jnp.exp(g_end) + _bdot(k * jnp.exp(g_end - gc), v, ((0,), (0,)))
        mu = jnp.mean(o, axis=-1, keepdims=True)
        var = jnp.mean(jnp.square(o - mu), axis=-1, keepdims=True)
        gate = g_ref[:, h * dv:(h + 1) * dv]
        o_ref[:, h * dv:(h + 1) * dv] = ((gate * jax.nn.sigmoid(gate)) * ((o - mu) * lax.rsqrt(var + B_GN_EPS))
                                         ).astype(o_ref.dtype)


def retention_chunks(proj, s0, lg, cnt, pos, B, L, H, dk, dv, C):
    assert (2 * dk) % dv == 0
    n = L // C
    hb = _divisor_tile(H, 4, 1)
    nb = H // hb
    lg_t = jnp.broadcast_to(lg.astype(F32)[:, None, None], (H, SUBLANES, LANES))
    inv = ROPE_BASE ** (-jnp.arange(0, dk, 2, dtype=F32) / dk)
    ang = pos.astype(F32)[:, None] * inv[None, :]
    v_sec = (2 * dk) // dv
    qk = lambda sec: pl.BlockSpec((C, hb * dk), lambda b, h, c: (b * n + c, sec * nb + h))
    vg = lambda sec: pl.BlockSpec((C, hb * dv), lambda b, h, c: (b * n + c, sec * nb + h))
    trig = pl.BlockSpec((C, dk // 2), lambda b, h, c: (c, 0))
    st = pl.BlockSpec((1, hb, dk, dv), lambda b, h, c: (b, h, 0, 0))
    return pl.pallas_call(
        functools.partial(_ret_kernel, C=C, hb=hb, dk=dk, dv=dv),
        out_shape=[jax.ShapeDtypeStruct((B * L, H * dv), BF16), jax.ShapeDtypeStruct((B, H, dk, dv), F32)],
        grid=(B, nb, n),
        in_specs=[pl.BlockSpec((hb, SUBLANES, LANES), lambda b, h, c: (h, 0, 0)),
                  pl.BlockSpec((C, 1), lambda b, h, c: (0, 0)),
                  pl.BlockSpec((1, C), lambda b, h, c: (0, 0)),
                  trig, trig, qk(0), qk(1), vg(v_sec), vg(v_sec + 1), st],
        out_specs=[pl.BlockSpec((C, hb * dv), lambda b, h, c: (b * n + c, h)), st],
        compiler_params=_params(("parallel", "parallel", "arbitrary")),
        name="retention",
    )(lg_t, cnt.reshape(C, 1), cnt.reshape(1, C), jnp.cos(ang), jnp.sin(ang), proj, proj, proj, proj, s0)


def _bdot(a, b, dims=((1,), (0,))):
    return lax.dot_general(a.astype(BF16), b.astype(BF16), (dims, ((), ())), preferred_element_type=F32)


def _unit_lower_inverses(ms, C):
    row = lax.broadcasted_iota(jnp.int32, (C, C), 0)
    col = lax.broadcasted_iota(jnp.int32, (C, C), 1)
    eye = jnp.where(row == col, 1.0, 0.0)
    ts = [eye - jnp.where((row >> 1) == (col >> 1), m, 0.0) for m in ms]
    sh = 1
    while (1 << sh) < C:
        same_big = (row >> (sh + 1)) == (col >> (sh + 1))
        same_small = (row >> sh) == (col >> sh)
        tl = [_bdot(t, jnp.where(same_big, jnp.where(same_small, 0.0, m), 0.0)) for t, m in zip(ts, ms)]
        ts = [t - _bdot(x, t) for t, x in zip(ts, tl)]
        sh += 1
    return ts


def _gdn_kernel(q_ref, k_ref, v_ref, gc_ref, gr_ref, b_ref, z_ref, ng_ref, s0_ref, o_ref, s_ref, *, C, nc):
    @pl.when(pl.program_id(2) == 0)
    def _():
        s_ref[...] = s0_ref[...]

    row = lax.broadcasted_iota(jnp.int32, (C, C), 0)
    col = lax.broadcasted_iota(jnp.int32, (C, C), 1)
    cs = range(nc)
    rows = [slice(c * C, (c + 1) * C) for c in cs]
    q = [q_ref[r, :] for r in rows]
    k = [k_ref[r, :] for r in rows]
    gc = [gc_ref[0, 0, r, :] for r in rows]
    beta = [b_ref[0, 0, r, :] for r in rows]
    dec = [jnp.exp(jnp.where(row >= col, gc[c] - gr_ref[0, 0, c], -jnp.inf)) for c in cs]
    kb = [k[c] * beta[c] for c in cs]
    ms = [jnp.where(row > col, _bdot(kb[c], k[c], ((1,), (1,))) * dec[c], 0.0) for c in cs]
    attn = [_bdot(q[c], k[c], ((1,), (1,))) * dec[c] for c in cs]
    ts = _unit_lower_inverses(ms, C)
    u = [_bdot(ts[c], v_ref[rows[c], :] * beta[c]) for c in cs]
    w = [_bdot(ts[c], kb[c] * jnp.exp(gc[c])) for c in cs]
    g_end = [gc[c][C - 1:C, :] for c in cs]
    kd = [k[c] * jnp.exp(g_end[c] - gc[c]) for c in cs]
    lhs = [jnp.concatenate([q[c] * jnp.exp(gc[c]) - _bdot(attn[c], w[c]), _bdot(kd[c], w[c], ((0,), (0,)))],
                           axis=0).astype(BF16) for c in cs]
    o_u = [_bdot(attn[c], u[c]) for c in cs]
    s_u = [_bdot(kd[c], u[c], ((0,), (0,))) for c in cs]
    S = s_ref[0, 0]
    for c in cs:
        ls = _bdot(lhs[c], S)
        o = ls[:C] + o_u[c]
        z = z_ref[rows[c], :]
        o = o * lax.rsqrt(jnp.mean(jnp.square(o), axis=-1, keepdims=True) + NORM_EPS) * ng_ref[...]
        o_ref[rows[c], :] = (o * (z * jax.nn.sigmoid(z))).astype(o_ref.dtype)
        S = S * jnp.exp(g_end[c]) - ls[C:] + s_u[c]
    s_ref[0, 0] = S


def gdn_chunks(q, k, v, g, beta, proj, z_col, norm_g, s0, B, L, H, d):
    C = _divisor_tile(L, 64, SUBLANES)
    n = L // C
    nc = _divisor_tile(n, 16, 1)
    G = jnp.cumsum(g.reshape(B, n, C, H), axis=2).transpose(0, 3, 1, 2)
    tok = pl.BlockSpec((nc * C, d), lambda b, h, c: (b * (n // nc) + c, h))
    z_tok = pl.BlockSpec((nc * C, d), lambda b, h, c: (b * (n // nc) + c, z_col // d + h))
    colv = pl.BlockSpec((1, 1, nc * C, 1), lambda b, h, c: (b, h, c, 0))
    st = pl.BlockSpec((1, 1, d, d), lambda b, h, c: (b, h, 0, 0))
    return pl.pallas_call(
        functools.partial(_gdn_kernel, C=C, nc=nc),
        out_shape=[jax.ShapeDtypeStruct((B * L, H * d), BF16), jax.ShapeDtypeStruct((B, H, d, d), F32)],
        grid=(B, H, n // nc),
        in_specs=[tok, tok, tok, colv, pl.BlockSpec((1, 1, nc, 1, C), lambda b, h, c: (b, h, c, 0, 0)), colv,
                  z_tok, pl.BlockSpec((1, d), lambda b, h, c: (0, 0)), st],
        out_specs=[tok, st],
        compiler_params=_params(("parallel", "parallel", "arbitrary")),
        name="gdn_chunks",
    )(q, k, v, G.reshape(B, H, L, 1), G.reshape(B, H, n, 1, C),
      beta.reshape(B, L, H).transpose(0, 2, 1).reshape(B, H, L, 1), proj, norm_g.reshape(1, d).astype(F32), s0)


def _rwkv_kernel(r_ref, k_ref, v_ref, a_ref, b_ref, g_ref, gate_ref, lnw_ref, lnb_ref, rk_ref, s0_ref, y_ref, s_ref,
                 *, C, nc, N):
    @pl.when(pl.program_id(2) == 0)
    def _():
        s_ref[...] = s0_ref[...]

    n_h = r_ref.shape[1] // N
    row = lax.broadcasted_iota(jnp.int32, (C, C), 0)
    col = lax.broadcasted_iota(jnp.int32, (C, C), 1)
    strict, incl = row > col, row >= col
    ones_lower = jnp.where(incl, 1.0, 0.0).astype(BF16)
    rows = [slice(c * C, (c + 1) * C) for c in range(nc)]
    at, rt, bh, kh, bt, kt, e_end = [], [], [], [], [], [], []
    for rs in rows:
        g = g_ref[rs, :]
        g_hi = g.astype(BF16)
        rem = g - g_hi.astype(F32)
        g_mid = rem.astype(BF16)
        g_lo = (rem - g_mid.astype(F32)).astype(BF16)
        G = sum(jnp.dot(ones_lower, p, preferred_element_type=F32) for p in (g_hi, g_mid, g_lo))
        e_neg = jnp.exp(-G)
        to_end = jnp.exp(G[C - 1:C, :] - G)
        b, k = b_ref[rs, :], k_ref[rs, :]
        at.append(a_ref[rs, :] * jnp.exp(G - g))
        rt.append(r_ref[rs, :] * jnp.exp(G))
        bh.append(b * e_neg)
        kh.append(k * e_neg)
        bt.append(b * to_end)
        kt.append(k * to_end)
        e_end.append(jnp.exp(G[C - 1:C, :]))
    chains = [(c, h) for c in range(nc) for h in range(n_h)]
    hl = lambda x, h: x[:, h * N:(h + 1) * N]
    nt = ((1,), (1,))
    tn = ((0,), (0,))
    a_ab = [jnp.where(strict, _bdot(hl(at[c], h), hl(bh[c], h), nt), 0.0) for c, h in chains]
    a_ak = [jnp.where(strict, _bdot(hl(at[c], h), hl(kh[c], h), nt), 0.0) for c, h in chains]
    a_rb = [jnp.where(incl, _bdot(hl(rt[c], h), hl(bh[c], h), nt), 0.0) for c, h in chains]
    a_rk = [jnp.where(incl, _bdot(hl(rt[c], h), hl(kh[c], h), nt), 0.0) for c, h in chains]
    v = [hl(v_ref[rows[c], :], h) for c, h in chains]
    ts = _unit_lower_inverses([-m for m in a_ab], C)
    t_a = [_bdot(ts[i], hl(at[c], h)) for i, (c, h) in enumerate(chains)]
    akv = [_bdot(a_ak[i], v[i]) for i in range(len(chains))]
    uv = [_bdot(ts[i], akv[i]) for i in range(len(chains))]
    lhs_y = [hl(rt[c], h) + _bdot(a_rb[i], t_a[i]) for i, (c, h) in enumerate(chains)]
    m2 = [_bdot(t_a[i], hl(bt[c], h), tn) for i, (c, h) in enumerate(chains)]
    y_c = [_bdot(a_rb[i], uv[i]) + _bdot(a_rk[i], v[i]) for i in range(len(chains))]
    s_c = [_bdot(uv[i], hl(bt[c], h), tn) + _bdot(v[i], hl(kt[c], h), tn) for i, (c, h) in enumerate(chains)]
    S = [s_ref[0, h] for h in range(n_h)]
    lanes = r_ref.shape[1]
    li = lax.broadcasted_iota(jnp.int32, (lanes, lanes), 0) // N
    lj = lax.broadcasted_iota(jnp.int32, (lanes, lanes), 1) // N
    same_head = jnp.where(li == lj, 1.0, 0.0).astype(BF16)

    def head_sum(x):
        hi = x.astype(BF16)
        lo = (x - hi.astype(F32)).astype(BF16)
        return (jnp.dot(hi, same_head, preferred_element_type=F32)
                + jnp.dot(lo, same_head, preferred_element_type=F32))

    ys = []
    for i, (c, h) in enumerate(chains):
        ys.append(_bdot(lhs_y[i], S[h], nt) + y_c[i])
        S[h] = S[h] * hl(e_end[c], h) + _bdot(S[h], m2[i]) + s_c[i]
    for h in range(n_h):
        s_ref[0, h] = S[h]
    y = [jnp.concatenate(ys[c * n_h:(c + 1) * n_h], axis=-1) for c in range(nc)]
    d = [y[c] - head_sum(y[c]) * (1.0 / N) for c in range(nc)]
    var = [head_sum(d[c] * d[c]) * (1.0 / N) for c in range(nc)]
    rkv = [head_sum(r_ref[rs, :] * k_ref[rs, :] * rk_ref[...]) * v_ref[rs, :] for rs in rows]
    for c, rs in enumerate(rows):
        yn = d[c] * lax.rsqrt(var[c] + A_GN_EPS) * lnw_ref[...] + lnb_ref[...]
        y_ref[rs, :] = ((yn + rkv[c]) * gate_ref[rs, :]).astype(y_ref.dtype)


def rwkv_chunks(r, k, v, a, b, g, gate, ln_w, ln_b, r_k, s0, B, L, H, N):
    C = _divisor_tile(L, 64, SUBLANES)
    n = L // C
    nc = _divisor_tile(n, 8, 1)
    n_h = LANES // N
    tok = pl.BlockSpec((nc * C, LANES), lambda b_, p, c: (b_ * (n // nc) + c, p))
    vec = pl.BlockSpec((1, LANES), lambda b_, p, c: (0, p))
    st = pl.BlockSpec((1, n_h, N, N), lambda b_, p, c: (b_, p, 0, 0))
    row = lambda t: t.reshape(1, H * N).astype(F32)
    return pl.pallas_call(
        functools.partial(_rwkv_kernel, C=C, nc=nc, N=N),
        out_shape=[jax.ShapeDtypeStruct((B * L, H * N), BF16), jax.ShapeDtypeStruct((B, H, N, N), F32)],
        grid=(B, H // n_h, n // nc),
        in_specs=[tok] * 7 + [vec] * 3 + [st],
        out_specs=[tok, st],
        compiler_params=_params(("parallel", "parallel", "arbitrary")),
        name="rwkv_chunks",
    )(r, k, v, a, b, g, gate, row(ln_w), row(ln_b), row(r_k), s0)


def _head_norm(y, eps):
    mu = jnp.mean(y, axis=-1, keepdims=True)
    var = jnp.mean(jnp.square(y - mu), axis=-1, keepdims=True)
    return (y - mu) * lax.rsqrt(var + eps)


def _rwkv(rows, h, shift_prev, s0, v_first, P, j):
    b_p, L, b_s, t_p = rows.b_p, rows.L, rows.b_s, rows.t_p
    D = h.shape[1]
    H, N = s0.shape[1], s0.shape[2]
    h_p = h[:t_p].reshape(b_p, L, D)
    prev = jnp.concatenate([jnp.concatenate([jnp.zeros((b_p, 1, D), F32), h_p[:, :-1]], axis=1).reshape(t_p, D),
                            shift_prev.astype(F32)], axis=0)
    new_shift_p, new_shift_s = h_p[:, -1], h[t_p:]
    mu = P['rwkv_mu'][j]
    xr, xk, xv, xw, xa, xg = ((h + (prev - h) * mu[s]).astype(BF16) for s in range(6))
    r = mm(xr, P['rwkv_w_rkv'], (j, 0))
    k = mm(xk, P['rwkv_w_rkv'], (j, 1))
    v = mm(xv, P['rwkv_w_rkv'], (j, 2))
    w_log = -jax.nn.softplus(-(P['rwkv_w0'][j] + mm(jnp.tanh(mm(xw, P['rwkv_w1'], (j,))), P['rwkv_w2'], (j,)))) - 0.5
    log_w = -jnp.exp(w_log)
    if v_first is None:
        v_first = v
    else:
        mix = jax.nn.sigmoid(P['rwkv_v0'][j - 1] + mm(mm(xv, P['rwkv_v1'], (j - 1,)), P['rwkv_v2'], (j - 1,)))
        v = v + (v_first - v) * mix
    a = jax.nn.sigmoid(P['rwkv_a0'][j] + mm(mm(xa, P['rwkv_a1'], (j,)), P['rwkv_a2'], (j,)))
    g = mm(jax.nn.sigmoid(mm(xg, P['rwkv_g1'], (j,))), P['rwkv_g2'], (j,))
    T = rows.T
    kk = (k * P['rwkv_k_k'][j]).reshape(T, H, N)
    kk = (kk / jnp.maximum(jnp.sqrt(jnp.sum(jnp.square(kk), axis=-1, keepdims=True)), 1e-12)).reshape(T, D)
    k = k * (1 + (a - 1) * P['rwkv_k_a'][j])
    na, nb = -kk, kk * a
    ln_w, ln_b, r_k = P['rwkv_ln_w'][j], P['rwkv_ln_b'][j], P['rwkv_r_k'][j]
    y_p, s_p = rwkv_chunks(r, k, v, na, nb, log_w, g, ln_w, ln_b, r_k, jnp.zeros((b_p,) + s0.shape[1:], F32),
                           b_p, L, H, N)
    rs, ks, vs = r[t_p:], k[t_p:], v[t_p:]
    y_s, s_s = run_dplr(rs, jnp.exp(log_w[t_p:]), ks, na[t_p:], nb[t_p:], vs, s0, b_s, 1, H, N, N, False)
    y_s = _head_norm(y_s.reshape(b_s, H, N), A_GN_EPS).reshape(b_s, D) * ln_w + ln_b
    bonus = jnp.sum((rs * ks).reshape(b_s, H, N) * r_k, axis=-1, keepdims=True) * vs.reshape(b_s, H, N)
    y_s = ((y_s + bonus.reshape(b_s, D)) * g[t_p:]).astype(BF16)
    out = mm(jnp.concatenate([y_p, y_s], axis=0), P['rwkv_w_o'], (j,))
    return out, (new_shift_p, new_shift_s), (s_p, s_s), v_first


def _retention(rows, h, s0, P, j):
    b_p, L, b_s, t_p, T = rows.b_p, rows.L, rows.b_s, rows.t_p, rows.T
    D = h.shape[1]
    H, dk, dv = s0.shape[1], s0.shape[2], s0.shape[3]
    proj = mm(h, P['ret_w_in'], (j,))
    lg = jnp.log1p(-jnp.exp2(-5.0 - jnp.arange(H, dtype=F32)))
    C = _divisor_tile(L, 256, SUBLANES)
    o_p, s_p = retention_chunks(proj, jnp.zeros((b_p,) + s0.shape[1:], F32), lg, jnp.arange(1, C + 1, dtype=F32),
                                jnp.arange(L), b_p, L, H, dk, dv, C)
    proj_s = jnp.pad(proj[t_p:, None, :], ((0, 0), (0, SUBLANES - 1), (0, 0))).reshape(b_s * SUBLANES, -1)
    o_s, s_s = retention_chunks(proj_s, s0.astype(F32), lg, jnp.ones((SUBLANES,), F32),
                                jnp.full((SUBLANES,), PAST_LEN), b_s, SUBLANES, H, dk, dv, SUBLANES)
    o = jnp.concatenate([o_p, o_s.reshape(b_s, SUBLANES, H * dv)[:, 0]], axis=0)
    out = mm(o, P['ret_w_o'], (j,))
    return out, (s_p, s_s)


def _gdn(rows, h, conv_buf, s0, P, j):
    b_p, L, b_s, t_p, T = rows.b_p, rows.L, rows.b_s, rows.t_p, rows.T
    D = h.shape[1]
    H, dh = s0.shape[1], s0.shape[2]
    n_conv = conv_buf.shape[1] + 1
    proj = mm(h, P['gdn_w_in'], (j,))
    qkv, z, a, b = jnp.split(proj, [3 * D, 4 * D, 4 * D + H], axis=-1)
    cat_p = jnp.concatenate([jnp.zeros((b_p, n_conv - 1, 3 * D), F32), qkv[:t_p].reshape(b_p, L, 3 * D)], axis=1)
    cat_s = jnp.concatenate([conv_buf.astype(F32), qkv[t_p:, None, :]], axis=1)
    new_buf = (cat_p[:, -(n_conv - 1):], cat_s[:, -(n_conv - 1):])
    cw = P['gdn_conv_w'][j]
    conv = lambda cat, n: sum(cat[:, i:i + n] * cw[i] for i in range(n_conv))
    qkv = jnp.concatenate([conv(cat_p, L).reshape(t_p, 3 * D), conv(cat_s, 1).reshape(b_s, 3 * D)], axis=0)
    qkv = jax.nn.silu(qkv)
    q, k, v = (t.reshape(T, H, dh) for t in jnp.split(qkv, 3, axis=-1))
    l2 = lambda x: x * lax.rsqrt(jnp.sum(jnp.square(x), axis=-1, keepdims=True) + 1e-6)
    q = l2(q) * dh ** -0.5
    k = l2(k)
    beta = jax.nn.sigmoid(b)
    log_alpha = -jnp.exp(P['gdn_a_log'][j].astype(F32)) * jax.nn.softplus(a + P['gdn_dt_bias'][j])
    o_p, s_p = gdn_chunks(q.reshape(T, D), k.reshape(T, D), v.reshape(T, D), log_alpha[:t_p], beta[:t_p],
                          proj, 3 * D, P['gdn_norm_g'][j], jnp.zeros((b_p,) + s0.shape[1:], F32), b_p, L, H, dh)
    qs, ks, vs = q[t_p:], k[t_p:], v[t_p:]
    al, be = jnp.exp(log_alpha[t_p:])[..., None], beta[t_p:, :, None]
    flat = lambda t: jnp.broadcast_to(t, (b_s, H, dh)).reshape(b_s, D)
    o_s, s_s = run_dplr(flat(qs), flat(al), flat(be * ks), flat(ks), flat(-al * be * ks), flat(vs), s0,
                        b_s, 1, H, dh, dh, True)
    o_s = o_s.reshape(b_s, H, dh)
    o_s = o_s * lax.rsqrt(jnp.mean(jnp.square(o_s), axis=-1, keepdims=True) + NORM_EPS) * P['gdn_norm_g'][j]
    o_s = (o_s * jax.nn.silu(z[t_p:].reshape(b_s, H, dh))).reshape(b_s, D).astype(BF16)
    out = mm(jnp.concatenate([o_p, o_s], axis=0), P['gdn_w_o'], (j,))
    return out, new_buf, (s_p, s_s)


def _router_kernel(l_ref, b_ref, i1_ref, i2_ref, p1_ref, p2_ref, *, n_e):
    logit = l_ref[...]
    s = logit + b_ref[...]
    lane = lax.broadcasted_iota(jnp.int32, s.shape, 1)
    first = lambda v: jnp.min(jnp.where(v == jnp.max(v, axis=-1, keepdims=True), lane, n_e), axis=-1, keepdims=True)
    i1 = first(s)
    i2 = first(jnp.where(lane == i1, -jnp.inf, s))
    pick = lambda i: jnp.sum(jnp.where(lane == i, logit, 0.0), axis=-1, keepdims=True)
    l1, l2 = pick(i1), pick(i2)
    m = jnp.maximum(l1, l2)
    e1, e2 = jnp.exp(l1 - m), jnp.exp(l2 - m)
    i1_ref[...] = i1
    i2_ref[...] = i2
    p1_ref[...] = e1 / (e1 + e2)
    p2_ref[...] = e2 / (e1 + e2)


def router_top2(logits, bias):
    T, n_e = logits.shape
    tr = _divisor_tile(T, 1040, SUBLANES)
    col = lambda: pl.BlockSpec((tr, 1), lambda i: (i, 0))
    return pl.pallas_call(
        functools.partial(_router_kernel, n_e=n_e),
        out_shape=[jax.ShapeDtypeStruct((T, 1), jnp.int32)] * 2 + [jax.ShapeDtypeStruct((T, 1), F32)] * 2,
        grid=(T // tr,),
        in_specs=[pl.BlockSpec((tr, n_e), lambda i: (i, 0)), pl.BlockSpec((1, n_e), lambda i: (0, 0))],
        out_specs=[col(), col(), col(), col()],
        compiler_params=_params(("parallel",)),
        name="router_top2",
    )(logits, bias.reshape(1, n_e).astype(F32))


def _gather_kernel(idx_ref, src_ref, o_ref, *scratch, tg):
    sem = scratch[-1]
    dst_ref = scratch[0] if len(scratch) == 2 else o_ref
    base = pl.program_id(0) * tg
    row_copy = lambda src_row, r: pltpu.make_async_copy(src_ref.at[pl.ds(src_row, 1)], dst_ref.at[pl.ds(r, 1)], sem)

    def start(r, c):
        row_copy(idx_ref[base + r], r).start()
        return c

    def wait(r, c):
        row_copy(0, r).wait()
        return c

    lax.fori_loop(0, tg, start, 0)
    lax.fori_loop(0, tg, wait, 0)
    if dst_ref is not o_ref:
        o_ref[...] = dst_ref[...].astype(o_ref.dtype)


def gather_rows(src, idx, tg, out_dtype=None):
    n_out = idx.shape[0]
    D = src.shape[1]
    out_dtype = out_dtype or src.dtype
    stage = [pltpu.VMEM((tg, D), src.dtype)] if out_dtype != src.dtype else []
    return pl.pallas_call(
        functools.partial(_gather_kernel, tg=tg),
        out_shape=jax.ShapeDtypeStruct((n_out, D), out_dtype),
        grid_spec=pltpu.PrefetchScalarGridSpec(
            num_scalar_prefetch=1, grid=(n_out // tg,),
            in_specs=[pl.BlockSpec(memory_space=pl.ANY)],
            out_specs=pl.BlockSpec((tg, D), lambda i, idx: (i, 0)),
            scratch_shapes=stage + [pltpu.SemaphoreType.DMA(())]),
        compiler_params=_params(("arbitrary",)),
        name="gather_rows",
    )(idx, src)


def _group_up_kernel(te_ref, ts_ref, ok_ref, a_ref, wg_ref, wu_ref, o_ref):
    ok = ok_ref[pl.program_id(0)] > 0

    @pl.when(ok)
    def _():
        _swiglu_up_kernel(a_ref, wg_ref, wu_ref, o_ref)

    @pl.when(jnp.logical_not(ok))
    def _():
        o_ref[...] = jnp.zeros_like(o_ref)


def _group_down_kernel(te_ref, ts_ref, ok_ref, a_ref, w_ref, o_ref, *scratch, nk, tk):
    ok = ok_ref[pl.program_id(0)] > 0

    @pl.when(ok)
    def _():
        _mm_kernel(a_ref, w_ref, o_ref, *scratch, nk=nk, tk=tk)

    @pl.when(jnp.logical_not(ok))
    def _():
        o_ref[...] = jnp.zeros_like(o_ref)


def grouped_swiglu_up(a, wg, wu, l, tiles, tm):
    M, K = a.shape
    F = wg.shape[-1]
    _, tn, _ = _mm_tiles(M, K, F, a.dtype.itemsize, 2, tm_fixed=tm)
    nj = F // tn
    w_spec = lambda: pl.BlockSpec((None, None, K, tn),
                                  lambda t, j, te, ts, ok: (l, te[t], 0, jnp.where(ok[t] > 0, j, nj - 1)))
    return pl.pallas_call(
        _group_up_kernel,
        out_shape=jax.ShapeDtypeStruct((M, F), BF16),
        grid_spec=pltpu.PrefetchScalarGridSpec(
            num_scalar_prefetch=3, grid=(M // tm, nj),
            in_specs=[pl.BlockSpec((tm, K), lambda t, j, te, ts, ok: (ts[t], 0)), w_spec(), w_spec()],
            out_specs=pl.BlockSpec((tm, tn), lambda t, j, te, ts, ok: (t, j))),
        compiler_params=_params(("arbitrary", "arbitrary")),
        name="grouped_swiglu_up",
    )(*tiles, a, wg, wu)


def grouped_mm(a, w, l, tiles, tm):
    M, K = a.shape
    N = w.shape[-1]
    _, tn, tk = _mm_tiles(M, K, N, a.dtype.itemsize, 1, tm_fixed=tm)
    nk, nj = K // tk, N // tn

    def w_map(t, j, k, te, ts, ok):
        live = ok[t] > 0
        return l, te[t], jnp.where(live, k, nk - 1), jnp.where(live, j, nj - 1)

    return pl.pallas_call(
        functools.partial(_group_down_kernel, nk=nk, tk=tk),
        out_shape=jax.ShapeDtypeStruct((M, N), F32),
        grid_spec=pltpu.PrefetchScalarGridSpec(
            num_scalar_prefetch=3, grid=(M // tm, nj, nk),
            in_specs=[pl.BlockSpec((tm, K), lambda t, j, k, te, ts, ok: (ts[t], 0)),
                      pl.BlockSpec((None, None, tk, tn), w_map)],
            out_specs=pl.BlockSpec((tm, tn), lambda t, j, k, te, ts, ok: (t, j)),
            scratch_shapes=[pltpu.VMEM((tm, tn), F32)] if nk > 1 else []),
        compiler_params=_params(("arbitrary", "arbitrary", "arbitrary")),
        name="grouped_mm",
    )(*tiles, a, w)


def _moe(rows, h, l, w_router, b_router, w_gate, w_up, w_down):
    T, D = h.shape
    n_e = w_router.shape[-1]
    i1, i2, p1, p2 = router_top2(mm(h, w_router, (l,)), b_router[l])
    tm = 768 if TOP_K * T >= 8 * 768 else 64
    n_tiles = pl.cdiv(TOP_K * T, tm) + n_e
    pair_e = jnp.concatenate([i1[:, 0], i2[:, 0]])
    onehot = (pair_e[:, None] == jnp.arange(n_e)[None, :]).astype(jnp.int32)
    rank = jnp.take_along_axis(jnp.cumsum(onehot, axis=0), pair_e[:, None], axis=1)[:, 0] - 1
    count = jnp.sum(onehot, axis=0)
    padded = (count + tm - 1) // tm * tm
    g_end = jnp.cumsum(padded)
    slot = ((g_end - padded)[pair_e] + rank).astype(jnp.int32)
    pair_tok = jnp.concatenate([jnp.arange(T, dtype=jnp.int32)] * TOP_K)
    src = jnp.zeros((n_tiles * tm,), jnp.int32).at[slot].set(pair_tok)
    n_used = g_end[-1] // tm
    t_id = jnp.minimum(jnp.arange(n_tiles), n_used - 1).astype(jnp.int32)
    tile_e = jnp.minimum(jnp.sum(t_id[:, None] * tm >= g_end[None, :], axis=1), n_e - 1).astype(jnp.int32)
    tile_ok = (jnp.arange(n_tiles) < n_used).astype(jnp.int32)
    tiles = (tile_e, t_id, tile_ok)
    hg = gather_rows(h, src, min(tm, 256), BF16)
    yg = grouped_mm(grouped_swiglu_up(hg, w_gate, w_up, l, tiles, tm), w_down, l, tiles, tm)
    return gather_rows(yg, slot, rows.tr), p1, p2


def kernel(x_prompt, x_sample, c_prompt, c_sample, state_rwkv_shift, state_rwkv_wkv, state_ret, state_gdn_conv,
           state_gdn, norm1_g, norm2_g, w_ada, b_ada, final_g, rwkv_mu, rwkv_w_rkv, rwkv_w0, rwkv_w1, rwkv_w2,
           rwkv_a0, rwkv_a1, rwkv_a2, rwkv_v0, rwkv_v1, rwkv_v2, rwkv_g1, rwkv_g2, rwkv_k_k, rwkv_k_a, rwkv_r_k,
           rwkv_ln_w, rwkv_ln_b, rwkv_w_o, ret_w_in, ret_w_o, gdn_w_in, gdn_conv_w, gdn_a_log, gdn_dt_bias,
           gdn_norm_g, gdn_w_o, ffn_w_gate, ffn_w_up, ffn_w_down, moe_w_router, moe_b_router, moe_w_gate,
           moe_w_up, moe_w_down):
    P = dict(rwkv_mu=rwkv_mu, rwkv_w_rkv=rwkv_w_rkv, rwkv_w0=rwkv_w0, rwkv_w1=rwkv_w1, rwkv_w2=rwkv_w2,
             rwkv_a0=rwkv_a0, rwkv_a1=rwkv_a1, rwkv_a2=rwkv_a2, rwkv_v0=rwkv_v0, rwkv_v1=rwkv_v1,
             rwkv_v2=rwkv_v2, rwkv_g1=rwkv_g1, rwkv_g2=rwkv_g2, rwkv_k_k=rwkv_k_k, rwkv_k_a=rwkv_k_a,
             rwkv_r_k=rwkv_r_k, rwkv_ln_w=rwkv_ln_w, rwkv_ln_b=rwkv_ln_b, rwkv_w_o=rwkv_w_o,
             ret_w_in=ret_w_in, ret_w_o=ret_w_o, gdn_w_in=gdn_w_in, gdn_conv_w=gdn_conv_w, gdn_a_log=gdn_a_log,
             gdn_dt_bias=gdn_dt_bias, gdn_norm_g=gdn_norm_g, gdn_w_o=gdn_w_o)
    b_p, L, D = x_prompt.shape
    b_s = x_sample.shape[0]
    depth = norm1_g.shape[0]
    rows = _Rows(b_p, L, b_s)
    t_p = rows.t_p
    x = jnp.concatenate([x_prompt.reshape(t_p, D), x_sample.reshape(b_s, D)], axis=0)
    c_act = jax.nn.silu(jnp.concatenate([c_prompt, c_sample], axis=0))
    mods = [(mm(c_act, w_ada, (i,)) + b_ada[i]).reshape(b_p + b_s, 6, D) for i in range(depth)]
    is_rwkv = lambda i: i % 3 == 0

    hs = resid_norm(rows, x, norm1_g[0], scale=mods[0][:, 1], shift=mods[0][:, 0],
                    h_dtypes=(F32,) if is_rwkv(0) else (BF16,))
    h = hs[0]
    shifts, wkvs, rets, convs, gdns = [], [], [], [], []
    v_first = None
    for i in range(depth):
        j = i // 3
        m = mods[i]
        if i % 3 == 0:
            out, s_shift, s_wkv, v_first = _rwkv(rows, h, state_rwkv_shift[j], state_rwkv_wkv[j], v_first, P, j)
            shifts.append(s_shift)
            wkvs.append(s_wkv)
        elif i % 3 == 1:
            out, s_ret = _retention(rows, h, state_ret[j], P, j)
            rets.append(s_ret)
        else:
            out, s_conv, s_gdn = _gdn(rows, h, state_gdn_conv[j], state_gdn[j], P, j)
            convs.append(s_conv)
            gdns.append(s_gdn)
        x, h = resid_norm(rows, x, norm2_g[i], y=out, gate=m[:, 2], scale=m[:, 4], shift=m[:, 3],
                          h_dtypes=(BF16,) if i % 2 == 0 else (F32,))
        if i % 2 == 0:
            f = dict(y=mm(swiglu_up(h, ffn_w_gate, ffn_w_up, (i // 2,)), ffn_w_down, (i // 2,)))
        else:
            f = dict(pair=_moe(rows, h, i // 2, moe_w_router, moe_b_router, moe_w_gate, moe_w_up, moe_w_down))
        if i + 1 < depth:
            mn = mods[i + 1]
            x, h = resid_norm(rows, x, norm1_g[i + 1], gate=m[:, 5], scale=mn[:, 1], shift=mn[:, 0],
                              h_dtypes=(F32,) if is_rwkv(i + 1) else (BF16,), **f)
        else:
            x, h = resid_norm(rows, x, final_g, gate=m[:, 5], h_dtypes=(F32,), **f)
    y = h
    stack = lambda pairs, k, dt: jnp.stack([p[k] for p in pairs]).astype(dt)
    return (y[:t_p].reshape(b_p, L, D), y[t_p:].reshape(b_s, 1, D),
            stack(shifts, 0, state_rwkv_shift.dtype), stack(shifts, 1, state_rwkv_shift.dtype),
            stack(wkvs, 0, state_rwkv_wkv.dtype), stack(wkvs, 1, state_rwkv_wkv.dtype),
            stack(rets, 0, state_ret.dtype), stack(rets, 1, state_ret.dtype),
            stack(convs, 0, state_gdn_conv.dtype), stack(convs, 1, state_gdn_conv.dtype),
            stack(gdns, 0, state_gdn.dtype), stack(gdns, 1, state_gdn.dtype))
```

```python
import functools
import math

import jax
import jax.numpy as jnp
from jax import lax
from jax.experimental import pallas as pl
from jax.experimental.pallas import tpu as pltpu

F32 = jnp.float32
BF16 = jnp.bfloat16

PAST_LEN = 16384
TOP_K = 2
NORM_EPS = 1e-6
A_GN_EPS = 64e-5
B_GN_EPS = 1e-6
ROPE_BASE = 10000.0

LANES = 128
SUBLANES = 8
VMEM_LIMIT = 56 * 2 ** 20
VMEM_BUDGET = 46 * 2 ** 20


def _divisor_tile(n, target, mult):
    best = None
    for t in range(mult, min(n, target) + 1, mult):
        if n % t == 0:
            best = t
    return best if best is not None else n


def _params(sem, **kw):
    return pltpu.CompilerParams(dimension_semantics=sem, vmem_limit_bytes=VMEM_LIMIT, **kw)


def _mm_kernel(a_ref, w_ref, o_ref, *scratch, nk, tk):
    if nk == 1:
        o_ref[...] = jnp.dot(a_ref[...].astype(BF16), w_ref[...].astype(BF16),
                             preferred_element_type=F32).astype(o_ref.dtype)
        return
    acc_ref, = scratch
    k = pl.program_id(2)
    a = a_ref[:, pl.ds(pl.multiple_of(k * tk, LANES), tk)]
    part = jnp.dot(a.astype(BF16), w_ref[...].astype(BF16), preferred_element_type=F32)

    @pl.when(k == 0)
    def _():
        acc_ref[...] = part

    @pl.when(k > 0)
    def _():
        acc_ref[...] += part

    @pl.when(k == nk - 1)
    def _():
        o_ref[...] = acc_ref[...].astype(o_ref.dtype)


def _mm_tiles(M, K, N, a_bytes, n_w, tm_fixed=None):
    tk = K if K <= 4096 else _divisor_tile(K, 6144, LANES)
    for tm_t, tn_t in ((1664, 512), (1040, 512), (832, 512), (640, 512), (1040, 256), (640, 256), (416, 256),
                       (208, 256), (104, 128), (8, 128)):
        tm = tm_fixed or _divisor_tile(M, tm_t, SUBLANES)
        if N % LANES == 0:
            tn = _divisor_tile(N, tn_t, LANES)
        else:
            tn = N if N <= tn_t else tn_t
        need = (2 * tm * K * a_bytes + n_w * (2 * tk * tn * 4 + tk * tn * 2) + tm * tk * 2 * (a_bytes == 4)
                + tm * tn * 4 * (2 + n_w + 1))
        if need <= VMEM_BUDGET:
            return tm, tn, tk
    return tm, tn, tk


def _w_spec(widx, tk, tn, imap):
    return pl.BlockSpec((None,) * len(widx) + (tk, tn), lambda *g: tuple(widx) + imap(*g))


def mm(a, w, widx=(), out_dtype=F32):
    M, K = a.shape
    N = w.shape[-1]
    tm, tn, tk = _mm_tiles(M, K, N, a.dtype.itemsize, 1)
    nk = K // tk
    grid = (M // tm, pl.cdiv(N, tn), nk)
    return pl.pallas_call(
        functools.partial(_mm_kernel, nk=nk, tk=tk),
        out_shape=jax.ShapeDtypeStruct((M, N), out_dtype),
        grid=grid,
        in_specs=[pl.BlockSpec((tm, K), lambda i, j, k: (i, 0)),
                  _w_spec(widx, tk, tn, lambda i, j, k: (k, j))],
        out_specs=pl.BlockSpec((tm, tn), lambda i, j, k: (i, j)),
        scratch_shapes=[pltpu.VMEM((tm, tn), F32)] if nk > 1 else [],
        compiler_params=_params(("parallel", "parallel", "arbitrary")),
        name="mm",
    )(a, w)


def _swiglu_up_kernel(a_ref, wg_ref, wu_ref, o_ref):
    a = a_ref[...].astype(BF16)
    g = jnp.dot(a, wg_ref[...].astype(BF16), preferred_element_type=F32)
    u = jnp.dot(a, wu_ref[...].astype(BF16), preferred_element_type=F32)
    o_ref[...] = (g * jax.nn.sigmoid(g) * u).astype(o_ref.dtype)


def swiglu_up(a, wg, wu, widx=()):
    M, K = a.shape
    F = wg.shape[-1]
    tm, tn, _ = _mm_tiles(M, K, F, a.dtype.itemsize, 2)
    return pl.pallas_call(
        _swiglu_up_kernel,
        out_shape=jax.ShapeDtypeStruct((M, F), BF16),
        grid=(M // tm, F // tn),
        in_specs=[pl.BlockSpec((tm, K), lambda i, j: (i, 0)),
                  _w_spec(widx, K, tn, lambda i, j: (0, j)),
                  _w_spec(widx, K, tn, lambda i, j: (0, j))],
        out_specs=pl.BlockSpec((tm, tn), lambda i, j: (i, j)),
        compiler_params=_params(("parallel", "parallel")),
        name="swiglu_up",
    )(a, wg, wu)


class _Rows:
    def __init__(self, b_p, L, b_s):
        self.b_p, self.L, self.b_s = b_p, L, b_s
        self.t_p = b_p * L
        self.T = self.t_p + b_s
        self.tr = min(LANES, math.gcd(L, b_s))
        assert self.tr % SUBLANES == 0
        self.n_pt = self.t_p // self.tr
        self.tiles_per_seq = L // self.tr
        self.n_tiles = self.T // self.tr

    def row_spec(self, D):
        return pl.BlockSpec((self.tr, D), lambda i: (i, 0))

    def mod_specs(self, D):
        n_pt, tps, b_p = self.n_pt, self.tiles_per_seq, self.b_p
        return [pl.BlockSpec((1, 1, D), lambda i: (jnp.minimum(i // tps, b_p - 1), 0, 0)),
                pl.BlockSpec((self.tr, D), lambda i: (jnp.maximum(i - n_pt, 0), 0))]

    def split_mod(self, m):
        return m[:self.b_p, None, :], m[self.b_p:]


def _resid_norm_kernel(*refs, n_pt, has_resid, has_mod, h_dtypes, n_mix, tiles_per_seq):
    it = iter(refs)
    x_ref = next(it)
    is_p = pl.program_id(0) < n_pt

    def mod():
        p_ref, s_ref = next(it), next(it)
        return jnp.where(is_p, p_ref[0], s_ref[...])

    x = x_ref[...]
    if has_resid == 1:
        gate = mod()
        x = x + gate * next(it)[...]
    elif has_resid == 2:
        gate = mod()
        y1_ref, y2_ref, p1_ref, p2_ref = next(it), next(it), next(it), next(it)
        x = x + gate * (p1_ref[...] * y1_ref[...] + p2_ref[...] * y2_ref[...])
    g_ref = next(it)
    if has_mod:
        sc, sh = mod(), mod()
    if n_mix:
        mu_ref, shift_ref = next(it), next(it)
    if has_resid:
        next(it)[...] = x
    h = x * lax.rsqrt(jnp.mean(jnp.square(x), axis=-1, keepdims=True) + NORM_EPS) * g_ref[...]
    if has_mod:
        h = h * (1 + sc) + sh
    for dt in h_dtypes:
        next(it)[...] = h.astype(dt)
    if n_mix:
        mix_refs = [next(it) for _ in range(n_mix)]
        carry_ref = next(it)
        tr = h.shape[0]

        @pl.when(pl.program_id(0) == 0)
        def _():
            carry_ref[...] = jnp.zeros_like(carry_ref)

        carry = jnp.where(pl.program_id(0) % tiles_per_seq == 0, 0.0, carry_ref[...])
        first_row = lax.broadcasted_iota(jnp.int32, h.shape, 0) == 0
        prev = jnp.where(is_p, jnp.where(first_row, carry, pltpu.roll(h, 1, 0)), shift_ref[...])
        carry_ref[...] = h[tr - 1:tr, :]
        d = prev - h
        for s, m_ref in enumerate(mix_refs):
            m_ref[...] = (h + d * mu_ref[s:s + 1, :]).astype(m_ref.dtype)


def resid_norm(rows, x, g, y=None, gate=None, scale=None, shift=None, h_dtypes=(BF16,), pair=None, mix=None):
    T, D = x.shape
    has_resid, has_mod = 2 if pair is not None else int(y is not None), scale is not None
    args, specs = [x], [rows.row_spec(D)]
    if has_resid == 1:
        args += [*rows.split_mod(gate), y]
        specs += [*rows.mod_specs(D), rows.row_spec(D)]
    elif has_resid == 2:
        y2, p1, p2 = pair
        n_t, tr = rows.n_tiles, rows.tr
        col = pl.BlockSpec((tr, 1), lambda i: (i, 0))
        args += [*rows.split_mod(gate), y2, y2, p1, p2]
        specs += [*rows.mod_specs(D), rows.row_spec(D), pl.BlockSpec((tr, D), lambda i: (i + n_t, 0)), col, col]
    args.append(g.reshape(1, D))
    specs.append(pl.BlockSpec((1, D), lambda i: (0, 0)))
    if has_mod:
        args += [*rows.split_mod(scale), *rows.split_mod(shift)]
        specs += [*rows.mod_specs(D), *rows.mod_specs(D)]
    n_mix = 0
    if mix is not None:
        mu, prev_s = mix
        n_mix, n_pt = mu.shape[0], rows.n_pt
        args += [mu.astype(F32), prev_s.astype(F32)]
        specs += [pl.BlockSpec((n_mix, D), lambda i: (0, 0)),
                  pl.BlockSpec((rows.tr, D), lambda i: (jnp.maximum(i - n_pt, 0), 0))]
    out_shape, out_specs = [], []
    if has_resid:
        out_shape.append(jax.ShapeDtypeStruct((T, D), F32))
        out_specs.append(rows.row_spec(D))
    for dt in tuple(h_dtypes) + (BF16,) * n_mix:
        out_shape.append(jax.ShapeDtypeStruct((T, D), dt))
        out_specs.append(rows.row_spec(D))
    return pl.pallas_call(
        functools.partial(_resid_norm_kernel, n_pt=rows.n_pt, has_resid=has_resid, has_mod=has_mod,
                          h_dtypes=tuple(h_dtypes), n_mix=n_mix, tiles_per_seq=rows.tiles_per_seq),
        out_shape=out_shape, grid=(rows.n_tiles,), in_specs=specs, out_specs=out_specs,
        scratch_shapes=[pltpu.VMEM((1, D), F32)] if n_mix else [],
        compiler_params=_params(("arbitrary",) if n_mix else ("parallel",)),
        name="resid_norm",
    )(*args)


def _ret_kernel(lg_ref, cc_ref, cr_ref, cos_ref, sin_ref, q_ref, k_ref, v_ref, g_ref, s0_ref, o_ref, s_ref,
                *, C, hb, dk, dv):
    @pl.when(pl.program_id(2) == 0)
    def _():
        s_ref[...] = s0_ref[...]

    row = lax.broadcasted_iota(jnp.int32, (C, C), 0)
    col = lax.broadcasted_iota(jnp.int32, (C, C), 1)
    causal = row >= col
    cos, sin = cos_ref[...], sin_ref[...]
    half = dk // 2

    def rotary(x):
        x1, x2 = x[:, :half], x[:, half:]
        return jnp.concatenate([x1 * cos - x2 * sin, x1 * sin + x2 * cos], axis=-1)

    for h in range(hb):
        lg = lg_ref[h, 0:1, 0:1]
        gc = lg * cc_ref[...]
        gr = lg * cr_ref[...]
        dec = jnp.where(causal, jnp.exp(jnp.where(causal, gc - gr, 0.0)), 0.0)
        q = rotary(q_ref[:, h * dk:(h + 1) * dk])
        k = rotary(k_ref[:, h * dk:(h + 1) * dk]) * dk ** -0.5
        v = v_ref[:, h * dv:(h + 1) * dv]
        S = s_ref[0, h]
        o = _bdot(_bdot(q, k, ((1,), (1,))) * dec, v) + _bdot(q * jnp.exp(gc), S)
        g_end = lg * cc_ref[C - 1:C, :]
        s_ref[0, h] = S * jnp.exp(g_end) + _bdot(k * jnp.exp(g_end - gc), v, ((0,), (0,)))
        mu = jnp.mean(o, axis=-1, keepdims=True)
        var = jnp.mean(jnp.square(o - mu), axis=-1, keepdims=True)
        gate = g_ref[:, h * dv:(h + 1) * dv]
        o_ref[:, h * dv:(h + 1) * dv] = ((gate * jax.nn.sigmoid(gate)) * ((o - mu) * lax.rsqrt(var + B_GN_EPS))
                                         ).astype(o_ref.dtype)


def retention_chunks(proj, s0, lg, cnt, pos, B, L, H, dk, dv, C):
    assert (2 * dk) % dv == 0
    n = L // C
    hb = _divisor_tile(H, 4, 1)
    nb = H // hb
    lg_t = jnp.broadcast_to(lg.astype(F32)[:, None, None], (H, SUBLANES, LANES))
    inv = ROPE_BASE ** (-jnp.arange(0, dk, 2, dtype=F32) / dk)
    ang = pos.astype(F32)[:, None] * inv[None, :]
    v_sec = (2 * dk) // dv
    qk = lambda sec: pl.BlockSpec((C, hb * dk), lambda b, h, c: (b * n + c, sec * nb + h))
    vg = lambda sec: pl.BlockSpec((C, hb * dv), lambda b, h, c: (b * n + c, sec * nb + h))
    trig = pl.BlockSpec((C, dk // 2), lambda b, h, c: (c, 0))
    st = pl.BlockSpec((1, hb, dk, dv), lambda b, h, c: (b, h, 0, 0))
    return pl.pallas_call(
        functools.partial(_ret_kernel, C=C, hb=hb, dk=dk, dv=dv),
        out_shape=[jax.ShapeDtypeStruct((B * L, H * dv), BF16), jax.ShapeDtypeStruct((B, H, dk, dv), F32)],
        grid=(B, nb, n),
        in_specs=[pl.BlockSpec((hb, SUBLANES, LANES), lambda b, h, c: (h, 0, 0)),
                  pl.BlockSpec((C, 1), lambda b, h, c: (0, 0)),
                  pl.BlockSpec((1, C), lambda b, h, c: (0, 0)),
                  trig, trig, qk(0), qk(1), vg(v_sec), vg(v_sec + 1), st],
        out_specs=[pl.BlockSpec((C, hb * dv), lambda b, h, c: (b * n + c, h)), st],
        compiler_params=_params(("parallel", "parallel", "arbitrary")),
        name="retention",
    )(lg_t, cnt.reshape(C, 1), cnt.reshape(1, C), jnp.cos(ang), jnp.sin(ang), proj, proj, proj, proj, s0)


def _bdot(a, b, dims=((1,), (0,))):
    return lax.dot_general(a.astype(BF16), b.astype(BF16), (dims, ((), ())), preferred_element_type=F32)


def _unit_lower_inverses(ms, C):
    row = lax.broadcasted_iota(jnp.int32, (C, C), 0)
    col = lax.broadcasted_iota(jnp.int32, (C, C), 1)
    eye = jnp.where(row == col, 1.0, 0.0)
    ts = [eye - jnp.where((row >> 1) == (col >> 1), m, 0.0) for m in ms]
    sh = 1
    while (1 << sh) < C:
        same_big = (row >> (sh + 1)) == (col >> (sh + 1))
        same_small = (row >> sh) == (col >> sh)
        tl = [_bdot(t, jnp.where(same_big, jnp.where(same_small, 0.0, m), 0.0)) for t, m in zip(ts, ms)]
        ts = [t - _bdot(x, t) for t, x in zip(ts, tl)]
        sh += 1
    return ts


def _gdn_kernel(q_ref, k_ref, v_ref, gc_ref, gr_ref, b_ref, z_ref, ng_ref, s0_ref, o_ref, s_ref, *, C, nc):
    @pl.when(pl.program_id(2) == 0)
    def _():
        s_ref[...] = s0_ref[...]

    row = lax.broadcasted_iota(jnp.int32, (C, C), 0)
    col = lax.broadcasted_iota(jnp.int32, (C, C), 1)
    cs = range(nc)
    rows = [slice(c * C, (c + 1) * C) for c in cs]
    q = [q_ref[r, :] for r in rows]
    k = [k_ref[r, :] for r in rows]
    gc = [gc_ref[0, 0, r, :] for r in rows]
    beta = [b_ref[0, 0, r, :] for r in rows]
    dec = [jnp.exp(jnp.where(row >= col, gc[c] - gr_ref[0, 0, c], -jnp.inf)) for c in cs]
    kb = [k[c] * beta[c] for c in cs]
    ms = [jnp.where(row > col, _bdot(kb[c], k[c], ((1,), (1,))) * dec[c], 0.0) for c in cs]
    attn = [_bdot(q[c], k[c], ((1,), (1,))) * dec[c] for c in cs]
    ts = _unit_lower_inverses(ms, C)
    u = [_bdot(ts[c], v_ref[rows[c], :] * beta[c]) for c in cs]
    w = [_bdot(ts[c], kb[c] * jnp.exp(gc[c])) for c in cs]
    g_end = [gc[c][C - 1:C, :] for c in cs]
    kd = [k[c] * jnp.exp(g_end[c] - gc[c]) for c in cs]
    lhs = [jnp.concatenate([q[c] * jnp.exp(gc[c]) - _bdot(attn[c], w[c]), _bdot(kd[c], w[c], ((0,), (0,)))],
                           axis=0).astype(BF16) for c in cs]
    o_u = [_bdot(attn[c], u[c]) for c in cs]
    s_u = [_bdot(kd[c], u[c], ((0,), (0,))) for c in cs]
    S = s_ref[0, 0]
    for c in cs:
        ls = _bdot(lhs[c], S)
        o = ls[:C] + o_u[c]
        z = z_ref[rows[c], :]
        o = o * lax.rsqrt(jnp.mean(jnp.square(o), axis=-1, keepdims=True) + NORM_EPS) * ng_ref[...]
        o_ref[rows[c], :] = (o * (z * jax.nn.sigmoid(z))).astype(o_ref.dtype)
        S = S * jnp.exp(g_end[c]) - ls[C:] + s_u[c]
    s_ref[0, 0] = S


def gdn_chunks(q, k, v, g, beta, proj, z_col, norm_g, s0, B, L, H, d):
    C = _divisor_tile(L, 64, SUBLANES)
    n = L // C
    nc = _divisor_tile(n, 16, 1)
    G = jnp.cumsum(g.reshape(B, n, C, H), axis=2).transpose(0, 3, 1, 2)
    tok = pl.BlockSpec((nc * C, d), lambda b, h, c: (b * (n // nc) + c, h))
    z_tok = pl.BlockSpec((nc * C, d), lambda b, h, c: (b * (n // nc) + c, z_col // d + h))
    colv = pl.BlockSpec((1, 1, nc * C, 1), lambda b, h, c: (b, h, c, 0))
    st = pl.BlockSpec((1, 1, d, d), lambda b, h, c: (b, h, 0, 0))
    return pl.pallas_call(
        functools.partial(_gdn_kernel, C=C, nc=nc),
        out_shape=[jax.ShapeDtypeStruct((B * L, H * d), BF16), jax.ShapeDtypeStruct((B, H, d, d), F32)],
        grid=(B, H, n // nc),
        in_specs=[tok, tok, tok, colv, pl.BlockSpec((1, 1, nc, 1, C), lambda b, h, c: (b, h, c, 0, 0)), colv,
                  z_tok, pl.BlockSpec((1, d), lambda b, h, c: (0, 0)), st],
        out_specs=[tok, st],
        compiler_params=_params(("parallel", "parallel", "arbitrary")),
        name="gdn_chunks",
    )(q, k, v, G.reshape(B, H, L, 1), G.reshape(B, H, n, 1, C),
      beta.reshape(B, L, H).transpose(0, 2, 1).reshape(B, H, L, 1), proj, norm_g.reshape(1, d).astype(F32), s0)


def _gdn_step_kernel(q_ref, k_ref, v_ref, al_ref, be_ref, s0_ref, o_ref, s_ref, *, hb, d):
    half = SUBLANES // 2
    top = lax.broadcasted_iota(jnp.int32, (SUBLANES, d), 0) < half
    row0 = lax.broadcasted_iota(jnp.int32, (SUBLANES, d), 0) == 0
    hs = range(hb)
    sls = [slice(h * d, (h + 1) * d) for h in hs]
    rep = lambda x: jnp.broadcast_to(x, (SUBLANES, d))
    q = [q_ref[0, :, sl] for sl in sls]
    k = [k_ref[0, :, sl] for sl in sls]
    alpha = [al_ref[0, :, sl] for sl in sls]
    S = [s0_ref[0, h] for h in hs]
    ks_qs = [_bdot(jnp.where(top, rep(k[h]), rep(q[h])), S[h]) for h in hs]
    v_new = [be_ref[0, :, sls[h]] * (v_ref[0, :, sls[h]] - alpha[h] * ks_qs[h][0:1, :]) for h in hs]
    kv = [_bdot(jnp.where(row0, rep(k[h]), 0.0), rep(v_new[h]), ((0,), (0,))) for h in hs]
    for h in hs:
        s_ref[0, h] = S[h] * alpha[h] + kv[h]
    o_ref[0] = jnp.concatenate([alpha[h] * ks_qs[h][half:half + 1, :]
                                + jnp.sum(q[h] * k[h], axis=-1, keepdims=True) * v_new[h] for h in hs], axis=-1)


def gdn_step(q, k, v, alpha, beta, s0, H, d):
    B = q.shape[0]
    hb = _divisor_tile(H, 8, 1)
    lanes = lambda t: jnp.broadcast_to(t[:, :, None], (B, H, d)).reshape(B, 1, H * d)
    vec = pl.BlockSpec((1, 1, hb * d), lambda b, h: (b, 0, h))
    st = pl.BlockSpec((1, hb, d, d), lambda b, h: (b, h, 0, 0))
    o, s = pl.pallas_call(
        functools.partial(_gdn_step_kernel, hb=hb, d=d),
        out_shape=[jax.ShapeDtypeStruct((B, 1, H * d), F32), jax.ShapeDtypeStruct((B, H, d, d), F32)],
        grid=(B, H // hb),
        in_specs=[vec] * 5 + [st],
        out_specs=[vec, st],
        compiler_params=_params(("parallel", "parallel")),
        name="gdn_step",
    )(q[:, None, :], k[:, None, :], v[:, None, :], lanes(alpha), lanes(beta), s0)
    return o[:, 0], s


def _rwkv_step_kernel(r_ref, w_ref, k_ref, v_ref, a_ref, b_ref, s0_ref, y_ref, s_ref, *, hb, N):
    rid = lax.broadcasted_iota(jnp.int32, (SUBLANES, N), 0)
    nt = ((1,), (1,))
    hs = range(hb)
    sls = [slice(h * N, (h + 1) * N) for h in hs]
    rep = lambda ref, h: jnp.broadcast_to(ref[0, :, sls[h]], (SUBLANES, N))
    pieces = lambda x: (x.astype(BF16), (x - x.astype(BF16).astype(F32)).astype(BF16))
    S = [s0_ref[0, h] for h in hs]
    sa = [_bdot(rep(a_ref, h), S[h], nt) for h in hs]
    left = [pieces(jnp.where(rid == 0, sa[h], jnp.where(rid == 1, rep(v_ref, h), 0.0))) for h in hs]
    right = [pieces(jnp.where(rid == 0, rep(b_ref, h), jnp.where(rid == 1, rep(k_ref, h), 0.0))) for h in hs]
    rank1 = [lax.dot_general(jnp.concatenate([left[h][0], left[h][0], left[h][1]], axis=0),
                             jnp.concatenate([right[h][0], right[h][1], right[h][0]], axis=0),
                             (((0,), (0,)), ((), ())), preferred_element_type=F32) for h in hs]
    S = [S[h] * w_ref[0, :, sls[h]] + rank1[h] for h in hs]
    y = [_bdot(rep(r_ref, h), S[h], nt)[0:1, :] for h in hs]
    for h in hs:
        s_ref[0, h] = S[h]
    y_ref[0] = jnp.concatenate(y, axis=-1)


def rwkv_step(r, w, k, v, a, b, s0, H, N):
    B = r.shape[0]
    hb = _divisor_tile(H, 16, 1)
    vec = pl.BlockSpec((1, 1, hb * N), lambda b_, h: (b_, 0, h))
    st = pl.BlockSpec((1, hb, N, N), lambda b_, h: (b_, h, 0, 0))
    y, s = pl.pallas_call(
        functools.partial(_rwkv_step_kernel, hb=hb, N=N),
        out_shape=[jax.ShapeDtypeStruct((B, 1, H * N), F32), jax.ShapeDtypeStruct((B, H, N, N), F32)],
        grid=(B, H // hb),
        in_specs=[vec] * 6 + [st],
        out_specs=[vec, st],
        compiler_params=_params(("parallel", "parallel")),
        name="rwkv_step",
    )(*(t[:, None, :] for t in (r, w, k, v, a, b)), s0)
    return y[:, 0], s


def _rwkv_kernel(r_ref, k_ref, v_ref, a_ref, b_ref, g_ref, gate_ref, lnw_ref, lnb_ref, rk_ref, s0_ref, y_ref, s_ref,
                 *, C, nc, N):
    @pl.when(pl.program_id(2) == 0)
    def _():
        s_ref[...] = s0_ref[...]

    n_h = r_ref.shape[1] // N
    row = lax.broadcasted_iota(jnp.int32, (C, C), 0)
    col = lax.broadcasted_iota(jnp.int32, (C, C), 1)
    strict, incl = row > col, row >= col
    ones_lower = jnp.where(incl, 1.0, 0.0).astype(BF16)
    rows = [slice(c * C, (c + 1) * C) for c in range(nc)]
    at, rt, bh, kh, bt, kt, e_end = [], [], [], [], [], [], []
    for rs in rows:
        g = g_ref[rs, :]
        g_hi = g.astype(BF16)
        rem = g - g_hi.astype(F32)
        g_mid = rem.astype(BF16)
        g_lo = (rem - g_mid.astype(F32)).astype(BF16)
        G = sum(jnp.dot(ones_lower, p, preferred_element_type=F32) for p in (g_hi, g_mid, g_lo))
        e_neg = jnp.exp(-G)
        to_end = jnp.exp(G[C - 1:C, :] - G)
        b, k = b_ref[rs, :], k_ref[rs, :]
        at.append(a_ref[rs, :] * jnp.exp(G - g))
        rt.append(r_ref[rs, :] * jnp.exp(G))
        bh.append(b * e_neg)
        kh.append(k * e_neg)
        bt.append(b * to_end)
        kt.append(k * to_end)
        e_end.append(jnp.exp(G[C - 1:C, :]))
    chains = [(c, h) for c in range(nc) for h in range(n_h)]
    hl = lambda x, h: x[:, h * N:(h + 1) * N]
    nt = ((1,), (1,))
    tn = ((0,), (0,))
    a_ab = [jnp.where(strict, _bdot(hl(at[c], h), hl(bh[c], h), nt), 0.0) for c, h in chains]
    a_ak = [jnp.where(strict, _bdot(hl(at[c], h), hl(kh[c], h), nt), 0.0) for c, h in chains]
    a_rb = [jnp.where(incl, _bdot(hl(rt[c], h), hl(bh[c], h), nt), 0.0) for c, h in chains]
    a_rk = [jnp.where(incl, _bdot(hl(rt[c], h), hl(kh[c], h), nt), 0.0) for c, h in chains]
    v = [hl(v_ref[rows[c], :], h) for c, h in chains]
    ts = _unit_lower_inverses([-m for m in a_ab], C)
    t_a = [_bdot(ts[i], hl(at[c], h)) for i, (c, h) in enumerate(chains)]
    akv = [_bdot(a_ak[i], v[i]) for i in range(len(chains))]
    uv = [_bdot(ts[i], akv[i]) for i in range(len(chains))]
    lhs_y = [hl(rt[c], h) + _bdot(a_rb[i], t_a[i]) for i, (c, h) in enumerate(chains)]
    m2 = [_bdot(t_a[i], hl(bt[c], h), tn) for i, (c, h) in enumerate(chains)]
    y_c = [_bdot(a_rb[i], uv[i]) + _bdot(a_rk[i], v[i]) for i in range(len(chains))]
    s_c = [_bdot(uv[i], hl(bt[c], h), tn) + _bdot(v[i], hl(kt[c], h), tn) for i, (c, h) in enumerate(chains)]
    S = [s_ref[0, h] for h in range(n_h)]
    lanes = r_ref.shape[1]
    li = lax.broadcasted_iota(jnp.int32, (lanes, lanes), 0) // N
    lj = lax.broadcasted_iota(jnp.int32, (lanes, lanes), 1) // N
    same_head = jnp.where(li == lj, 1.0, 0.0).astype(BF16)

    def head_sum(x):
        hi = x.astype(BF16)
        lo = (x - hi.astype(F32)).astype(BF16)
        return (jnp.dot(hi, same_head, preferred_element_type=F32)
                + jnp.dot(lo, same_head, preferred_element_type=F32))

    ys = []
    for i, (c, h) in enumerate(chains):
        ys.append(_bdot(lhs_y[i], S[h], nt) + y_c[i])
        S[h] = S[h] * hl(e_end[c], h) + _bdot(S[h], m2[i]) + s_c[i]
    for h in range(n_h):
        s_ref[0, h] = S[h]
    y = [jnp.concatenate(ys[c * n_h:(c + 1) * n_h], axis=-1) for c in range(nc)]
    d = [y[c] - head_sum(y[c]) * (1.0 / N) for c in range(nc)]
    var = [head_sum(d[c] * d[c]) * (1.0 / N) for c in range(nc)]
    rkv = [head_sum(r_ref[rs, :] * k_ref[rs, :] * rk_ref[...]) * v_ref[rs, :] for rs in rows]
    for c, rs in enumerate(rows):
        yn = d[c] * lax.rsqrt(var[c] + A_GN_EPS) * lnw_ref[...] + lnb_ref[...]
        y_ref[rs, :] = ((yn + rkv[c]) * gate_ref[rs, :]).astype(y_ref.dtype)


def rwkv_chunks(r, k, v, a, b, g, gate, ln_w, ln_b, r_k, s0, B, L, H, N):
    C = _divisor_tile(L, 64, SUBLANES)
    n = L // C
    nc = _divisor_tile(n, 8, 1)
    n_h = LANES // N
    tok = pl.BlockSpec((nc * C, LANES), lambda b_, p, c: (b_ * (n // nc) + c, p))
    vec = pl.BlockSpec((1, LANES), lambda b_, p, c: (0, p))
    st = pl.BlockSpec((1, n_h, N, N), lambda b_, p, c: (b_, p, 0, 0))
    row = lambda t: t.reshape(1, H * N).astype(F32)
    return pl.pallas_call(
        functools.partial(_rwkv_kernel, C=C, nc=nc, N=N),
        out_shape=[jax.ShapeDtypeStruct((B * L, H * N), BF16), jax.ShapeDtypeStruct((B, H, N, N), F32)],
        grid=(B, H // n_h, n // nc),
        in_specs=[tok] * 7 + [vec] * 3 + [st],
        out_specs=[tok, st],
        compiler_params=_params(("parallel", "parallel", "arbitrary")),
        name="rwkv_chunks",
    )(r, k, v, a, b, g, gate, row(ln_w), row(ln_b), row(r_k), s0)


def _head_norm(y, eps):
    mu = jnp.mean(y, axis=-1, keepdims=True)
    var = jnp.mean(jnp.square(y - mu), axis=-1, keepdims=True)
    return (y - mu) * lax.rsqrt(var + eps)


def _rwkv(rows, h, mixes, s0, v_first, P, j):
    b_p, L, b_s, t_p = rows.b_p, rows.L, rows.b_s, rows.t_p
    D = h.shape[1]
    H, N = s0.shape[1], s0.shape[2]
    new_shift_p, new_shift_s = h[:t_p].reshape(b_p, L, D)[:, -1], h[t_p:]
    xr, xk, xv, xw, xa, xg = mixes
    r = mm(xr, P['rwkv_w_rkv'], (j, 0))
    k = mm(xk, P['rwkv_w_rkv'], (j, 1))
    v = mm(xv, P['rwkv_w_rkv'], (j, 2))
    w_log = -jax.nn.softplus(-(P['rwkv_w0'][j] + mm(jnp.tanh(mm(xw, P['rwkv_w1'], (j,))), P['rwkv_w2'], (j,)))) - 0.5
    log_w = -jnp.exp(w_log)
    if v_first is None:
        v_first = v
    else:
        mix = jax.nn.sigmoid(P['rwkv_v0'][j - 1] + mm(mm(xv, P['rwkv_v1'], (j - 1,)), P['rwkv_v2'], (j - 1,)))
        v = v + (v_first - v) * mix
    a = jax.nn.sigmoid(P['rwkv_a0'][j] + mm(mm(xa, P['rwkv_a1'], (j,)), P['rwkv_a2'], (j,)))
    g = mm(jax.nn.sigmoid(mm(xg, P['rwkv_g1'], (j,))), P['rwkv_g2'], (j,))
    T = rows.T
    kk = (k * P['rwkv_k_k'][j]).reshape(T, H, N)
    kk = (kk / jnp.maximum(jnp.sqrt(jnp.sum(jnp.square(kk), axis=-1, keepdims=True)), 1e-12)).reshape(T, D)
    k = k * (1 + (a - 1) * P['rwkv_k_a'][j])
    na, nb = -kk, kk * a
    ln_w, ln_b, r_k = P['rwkv_ln_w'][j], P['rwkv_ln_b'][j], P['rwkv_r_k'][j]
    y_p, s_p = rwkv_chunks(r, k, v, na, nb, log_w, g, ln_w, ln_b, r_k, jnp.zeros((b_p,) + s0.shape[1:], F32),
                           b_p, L, H, N)
    rs, ks, vs = r[t_p:], k[t_p:], v[t_p:]
    y_s, s_s = rwkv_step(rs, jnp.exp(log_w[t_p:]), ks, vs, na[t_p:], nb[t_p:], s0.astype(F32), H, N)
    y_s = _head_norm(y_s.reshape(b_s, H, N), A_GN_EPS).reshape(b_s, D) * ln_w + ln_b
    bonus = jnp.sum((rs * ks).reshape(b_s, H, N) * r_k, axis=-1, keepdims=True) * vs.reshape(b_s, H, N)
    y_s = ((y_s + bonus.reshape(b_s, D)) * g[t_p:]).astype(BF16)
    out = mm(jnp.concatenate([y_p, y_s], axis=0), P['rwkv_w_o'], (j,))
    return out, (new_shift_p, new_shift_s), (s_p, s_s), v_first


def _retention(rows, h, s0, P, j):
    b_p, L, b_s, t_p, T = rows.b_p, rows.L, rows.b_s, rows.t_p, rows.T
    D = h.shape[1]
    H, dk, dv = s0.shape[1], s0.shape[2], s0.shape[3]
    proj = mm(h, P['ret_w_in'], (j,))
    lg = jnp.log1p(-jnp.exp2(-5.0 - jnp.arange(H, dtype=F32)))
    C = _divisor_tile(L, 256, SUBLANES)
    o_p, s_p = retention_chunks(proj, jnp.zeros((b_p,) + s0.shape[1:], F32), lg, jnp.arange(1, C + 1, dtype=F32),
                                jnp.arange(L), b_p, L, H, dk, dv, C)
    proj_s = jnp.pad(proj[t_p:, None, :], ((0, 0), (0, SUBLANES - 1), (0, 0))).reshape(b_s * SUBLANES, -1)
    o_s, s_s = retention_chunks(proj_s, s0.astype(F32), lg, jnp.ones((SUBLANES,), F32),
                                jnp.full((SUBLANES,), PAST_LEN), b_s, SUBLANES, H, dk, dv, SUBLANES)
    o = jnp.concatenate([o_p, o_s.reshape(b_s, SUBLANES, H * dv)[:, 0]], axis=0)
    out = mm(o, P['ret_w_o'], (j,))
    return out, (s_p, s_s)


def _gdn(rows, h, conv_buf, s0, P, j):
    b_p, L, b_s, t_p, T = rows.b_p, rows.L, rows.b_s, rows.t_p, rows.T
    D = h.shape[1]
    H, dh = s0.shape[1], s0.shape[2]
    n_conv = conv_buf.shape[1] + 1
    proj = mm(h, P['gdn_w_in'], (j,))
    qkv, z, a, b = jnp.split(proj, [3 * D, 4 * D, 4 * D + H], axis=-1)
    cat_p = jnp.concatenate([jnp.zeros((b_p, n_conv - 1, 3 * D), F32), qkv[:t_p].reshape(b_p, L, 3 * D)], axis=1)
    cat_s = jnp.concatenate([conv_buf.astype(F32), qkv[t_p:, None, :]], axis=1)
    new_buf = (cat_p[:, -(n_conv - 1):], cat_s[:, -(n_conv - 1):])
    cw = P['gdn_conv_w'][j]
    conv = lambda cat, n: sum(cat[:, i:i + n] * cw[i] for i in range(n_conv))
    qkv = jnp.concatenate([conv(cat_p, L).reshape(t_p, 3 * D), conv(cat_s, 1).reshape(b_s, 3 * D)], axis=0)
    qkv = jax.nn.silu(qkv)
    q, k, v = (t.reshape(T, H, dh) for t in jnp.split(qkv, 3, axis=-1))
    l2 = lambda x: x * lax.rsqrt(jnp.sum(jnp.square(x), axis=-1, keepdims=True) + 1e-6)
    q = l2(q) * dh ** -0.5
    k = l2(k)
    beta = jax.nn.sigmoid(b)
    log_alpha = -jnp.exp(P['gdn_a_log'][j].astype(F32)) * jax.nn.softplus(a + P['gdn_dt_bias'][j])
    o_p, s_p = gdn_chunks(q.reshape(T, D), k.reshape(T, D), v.reshape(T, D), log_alpha[:t_p], beta[:t_p],
                          proj, 3 * D, P['gdn_norm_g'][j], jnp.zeros((b_p,) + s0.shape[1:], F32), b_p, L, H, dh)
    o_s, s_s = gdn_step(q[t_p:].reshape(b_s, D), k[t_p:].reshape(b_s, D), v[t_p:].reshape(b_s, D),
                        jnp.exp(log_alpha[t_p:]), beta[t_p:], s0.astype(F32), H, dh)
    o_s = o_s.reshape(b_s, H, dh)
    o_s = o_s * lax.rsqrt(jnp.mean(jnp.square(o_s), axis=-1, keepdims=True) + NORM_EPS) * P['gdn_norm_g'][j]
    o_s = (o_s * jax.nn.silu(z[t_p:].reshape(b_s, H, dh))).reshape(b_s, D).astype(BF16)
    out = mm(jnp.concatenate([o_p, o_s], axis=0), P['gdn_w_o'], (j,))
    return out, new_buf, (s_p, s_s)


def _router_kernel(l_ref, b_ref, i1_ref, i2_ref, p1_ref, p2_ref, *, n_e):
    logit = l_ref[...]
    s = logit + b_ref[...]
    lane = lax.broadcasted_iota(jnp.int32, s.shape, 1)
    first = lambda v: jnp.min(jnp.where(v == jnp.max(v, axis=-1, keepdims=True), lane, n_e), axis=-1, keepdims=True)
    i1 = first(s)
    i2 = first(jnp.where(lane == i1, -jnp.inf, s))
    pick = lambda i: jnp.sum(jnp.where(lane == i, logit, 0.0), axis=-1, keepdims=True)
    l1, l2 = pick(i1), pick(i2)
    m = jnp.maximum(l1, l2)
    e1, e2 = jnp.exp(l1 - m), jnp.exp(l2 - m)
    i1_ref[...] = i1
    i2_ref[...] = i2
    p1_ref[...] = e1 / (e1 + e2)
    p2_ref[...] = e2 / (e1 + e2)


def router_top2(logits, bias):
    T, n_e = logits.shape
    tr = _divisor_tile(T, 1040, SUBLANES)
    col = lambda: pl.BlockSpec((tr, 1), lambda i: (i, 0))
    return pl.pallas_call(
        functools.partial(_router_kernel, n_e=n_e),
        out_shape=[jax.ShapeDtypeStruct((T, 1), jnp.int32)] * 2 + [jax.ShapeDtypeStruct((T, 1), F32)] * 2,
        grid=(T // tr,),
        in_specs=[pl.BlockSpec((tr, n_e), lambda i: (i, 0)), pl.BlockSpec((1, n_e), lambda i: (0, 0))],
        out_specs=[col(), col(), col(), col()],
        compiler_params=_params(("parallel",)),
        name="router_top2",
    )(logits, bias.reshape(1, n_e).astype(F32))


def _gather_kernel(idx_ref, src_ref, o_ref, *scratch, tg):
    sem = scratch[-1]
    dst_ref = scratch[0] if len(scratch) == 2 else o_ref
    base = pl.program_id(0) * tg
    row_copy = lambda src_row, r: pltpu.make_async_copy(src_ref.at[pl.ds(src_row, 1)], dst_ref.at[pl.ds(r, 1)], sem)

    def start(r, c):
        row_copy(idx_ref[base + r], r).start()
        return c

    def wait(r, c):
        row_copy(0, r).wait()
        return c

    lax.fori_loop(0, tg, start, 0)
    lax.fori_loop(0, tg, wait, 0)
    if dst_ref is not o_ref:
        o_ref[...] = dst_ref[...].astype(o_ref.dtype)


def gather_rows(src, idx, tg, out_dtype=None):
    n_out = idx.shape[0]
    D = src.shape[1]
    out_dtype = out_dtype or src.dtype
    stage = [pltpu.VMEM((tg, D), src.dtype)] if out_dtype != src.dtype else []
    return pl.pallas_call(
        functools.partial(_gather_kernel, tg=tg),
        out_shape=jax.ShapeDtypeStruct((n_out, D), out_dtype),
        grid_spec=pltpu.PrefetchScalarGridSpec(
            num_scalar_prefetch=1, grid=(n_out // tg,),
            in_specs=[pl.BlockSpec(memory_space=pl.ANY)],
            out_specs=pl.BlockSpec((tg, D), lambda i, idx: (i, 0)),
            scratch_shapes=stage + [pltpu.SemaphoreType.DMA(())]),
        compiler_params=_params(("arbitrary",), disable_bounds_checks=True),
        name="gather_rows",
    )(idx, src)


def _group_up_kernel(te_ref, ts_ref, ok_ref, a_ref, wg_ref, wu_ref, o_ref):
    ok = ok_ref[pl.program_id(0)] > 0

    @pl.when(ok)
    def _():
        _swiglu_up_kernel(a_ref, wg_ref, wu_ref, o_ref)

    @pl.when(jnp.logical_not(ok))
    def _():
        o_ref[...] = jnp.zeros_like(o_ref)


def _group_down_kernel(te_ref, ts_ref, ok_ref, a_ref, w_ref, o_ref, *scratch, nk, tk):
    ok = ok_ref[pl.program_id(0)] > 0

    @pl.when(ok)
    def _():
        _mm_kernel(a_ref, w_ref, o_ref, *scratch, nk=nk, tk=tk)

    @pl.when(jnp.logical_not(ok))
    def _():
        o_ref[...] = jnp.zeros_like(o_ref)


def grouped_swiglu_up(a, wg, wu, l, tiles, tm):
    M, K = a.shape
    F = wg.shape[-1]
    _, tn, _ = _mm_tiles(M, K, F, a.dtype.itemsize, 2, tm_fixed=tm)
    nj = F // tn
    w_spec = lambda: pl.BlockSpec((None, None, K, tn),
                                  lambda t, j, te, ts, ok: (l, te[t], 0, jnp.where(ok[t] > 0, j, nj - 1)))
    return pl.pallas_call(
        _group_up_kernel,
        out_shape=jax.ShapeDtypeStruct((M, F), BF16),
        grid_spec=pltpu.PrefetchScalarGridSpec(
            num_scalar_prefetch=3, grid=(M // tm, nj),
            in_specs=[pl.BlockSpec((tm, K), lambda t, j, te, ts, ok: (ts[t], 0)), w_spec(), w_spec()],
            out_specs=pl.BlockSpec((tm, tn), lambda t, j, te, ts, ok: (t, j))),
        compiler_params=_params(("arbitrary", "arbitrary")),
        name="grouped_swiglu_up",
    )(*tiles, a, wg, wu)


def grouped_mm(a, w, l, tiles, tm):
    M, K = a.shape
    N = w.shape[-1]
    _, tn, tk = _mm_tiles(M, K, N, a.dtype.itemsize, 1, tm_fixed=tm)
    nk, nj = K // tk, N // tn

    def w_map(t, j, k, te, ts, ok):
        live = ok[t] > 0
        return l, te[t], jnp.where(live, k, nk - 1), jnp.where(live, j, nj - 1)

    return pl.pallas_call(
        functools.partial(_group_down_kernel, nk=nk, tk=tk),
        out_shape=jax.ShapeDtypeStruct((M, N), F32),
        grid_spec=pltpu.PrefetchScalarGridSpec(
            num_scalar_prefetch=3, grid=(M // tm, nj, nk),
            in_specs=[pl.BlockSpec((tm, K), lambda t, j, k, te, ts, ok: (ts[t], 0)),
                      pl.BlockSpec((None, None, tk, tn), w_map)],
            out_specs=pl.BlockSpec((tm, tn), lambda t, j, k, te, ts, ok: (t, j)),
            scratch_shapes=[pltpu.VMEM((tm, tn), F32)] if nk > 1 else []),
        compiler_params=_params(("arbitrary", "arbitrary", "arbitrary")),
        name="grouped_mm",
    )(*tiles, a, w)


def _moe(rows, h, l, w_router, b_router, w_gate, w_up, w_down):
    T, D = h.shape
    n_e = w_router.shape[-1]
    i1, i2, p1, p2 = router_top2(mm(h, w_router, (l,)), b_router[l])
    tm = 768 if TOP_K * T >= 8 * 768 else 64
    n_tiles = pl.cdiv(TOP_K * T, tm) + n_e
    pair_e = jnp.concatenate([i1[:, 0], i2[:, 0]])
    onehot = (pair_e[:, None] == jnp.arange(n_e)[None, :]).astype(jnp.int32)
    rank = jnp.take_along_axis(jnp.cumsum(onehot, axis=0), pair_e[:, None], axis=1)[:, 0] - 1
    count = jnp.sum(onehot, axis=0)
    padded = (count + tm - 1) // tm * tm
    g_end = jnp.cumsum(padded)
    slot = ((g_end - padded)[pair_e] + rank).astype(jnp.int32)
    pair_tok = jnp.concatenate([jnp.arange(T, dtype=jnp.int32)] * TOP_K)
    src = jnp.zeros((n_tiles * tm,), jnp.int32).at[slot].set(pair_tok)
    n_used = g_end[-1] // tm
    t_id = jnp.minimum(jnp.arange(n_tiles), n_used - 1).astype(jnp.int32)
    tile_e = jnp.minimum(jnp.sum(t_id[:, None] * tm >= g_end[None, :], axis=1), n_e - 1).astype(jnp.int32)
    tile_ok = (jnp.arange(n_tiles) < n_used).astype(jnp.int32)
    tiles = (tile_e, t_id, tile_ok)
    hg = gather_rows(h, src, min(tm, 256), BF16)
    yg = grouped_mm(grouped_swiglu_up(hg, w_gate, w_up, l, tiles, tm), w_down, l, tiles, tm)
    return gather_rows(yg, slot, rows.tr), p1, p2


def kernel(x_prompt, x_sample, c_prompt, c_sample, state_rwkv_shift, state_rwkv_wkv, state_ret, state_gdn_conv,
           state_gdn, norm1_g, norm2_g, w_ada, b_ada, final_g, rwkv_mu, rwkv_w_rkv, rwkv_w0, rwkv_w1, rwkv_w2,
           rwkv_a0, rwkv_a1, rwkv_a2, rwkv_v0, rwkv_v1, rwkv_v2, rwkv_g1, rwkv_g2, rwkv_k_k, rwkv_k_a, rwkv_r_k,
           rwkv_ln_w, rwkv_ln_b, rwkv_w_o, ret_w_in, ret_w_o, gdn_w_in, gdn_conv_w, gdn_a_log, gdn_dt_bias,
           gdn_norm_g, gdn_w_o, ffn_w_gate, ffn_w_up, ffn_w_down, moe_w_router, moe_b_router, moe_w_gate,
           moe_w_up, moe_w_down):
    P = dict(rwkv_mu=rwkv_mu, rwkv_w_rkv=rwkv_w_rkv, rwkv_w0=rwkv_w0, rwkv_w1=rwkv_w1, rwkv_w2=rwkv_w2,
             rwkv_a0=rwkv_a0, rwkv_a1=rwkv_a1, rwkv_a2=rwkv_a2, rwkv_v0=rwkv_v0, rwkv_v1=rwkv_v1,
             rwkv_v2=rwkv_v2, rwkv_g1=rwkv_g1, rwkv_g2=rwkv_g2, rwkv_k_k=rwkv_k_k, rwkv_k_a=rwkv_k_a,
             rwkv_r_k=rwkv_r_k, rwkv_ln_w=rwkv_ln_w, rwkv_ln_b=rwkv_ln_b, rwkv_w_o=rwkv_w_o,
             ret_w_in=ret_w_in, ret_w_o=ret_w_o, gdn_w_in=gdn_w_in, gdn_conv_w=gdn_conv_w, gdn_a_log=gdn_a_log,
             gdn_dt_bias=gdn_dt_bias, gdn_norm_g=gdn_norm_g, gdn_w_o=gdn_w_o)
    b_p, L, D = x_prompt.shape
    b_s = x_sample.shape[0]
    depth = norm1_g.shape[0]
    rows = _Rows(b_p, L, b_s)
    t_p = rows.t_p
    x = jnp.concatenate([x_prompt.reshape(t_p, D), x_sample.reshape(b_s, D)], axis=0)
    c_act = jax.nn.silu(jnp.concatenate([c_prompt, c_sample], axis=0))
    mods = [(mm(c_act, w_ada, (i,)) + b_ada[i]).reshape(b_p + b_s, 6, D) for i in range(depth)]

    def mixer_input(i):
        if i % 3 != 0:
            return dict(h_dtypes=(BF16,))
        return dict(h_dtypes=(F32,), mix=(rwkv_mu[i // 3], state_rwkv_shift[i // 3]))

    h, *mixes = resid_norm(rows, x, norm1_g[0], scale=mods[0][:, 1], shift=mods[0][:, 0], **mixer_input(0))
    shifts, wkvs, rets, convs, gdns = [], [], [], [], []
    v_first = None
    for i in range(depth):
        j = i // 3
        m = mods[i]
        if i % 3 == 0:
            out, s_shift, s_wkv, v_first = _rwkv(rows, h, mixes, state_rwkv_wkv[j], v_first, P, j)
            shifts.append(s_shift)
            wkvs.append(s_wkv)
        elif i % 3 == 1:
            out, s_ret = _retention(rows, h, state_ret[j], P, j)
            rets.append(s_ret)
        else:
            out, s_conv, s_gdn = _gdn(rows, h, state_gdn_conv[j], state_gdn[j], P, j)
            convs.append(s_conv)
            gdns.append(s_gdn)
        x, h = resid_norm(rows, x, norm2_g[i], y=out, gate=m[:, 2], scale=m[:, 4], shift=m[:, 3],
                          h_dtypes=(BF16,) if i % 2 == 0 else (F32,))
        if i % 2 == 0:
            f = dict(y=mm(swiglu_up(h, ffn_w_gate, ffn_w_up, (i // 2,)), ffn_w_down, (i // 2,)))
        else:
            f = dict(pair=_moe(rows, h, i // 2, moe_w_router, moe_b_router, moe_w_gate, moe_w_up, moe_w_down))
        if i + 1 < depth:
            mn = mods[i + 1]
            x, h, *mixes = resid_norm(rows, x, norm1_g[i + 1], gate=m[:, 5], scale=mn[:, 1], shift=mn[:, 0],
                                      **mixer_input(i + 1), **f)
        else:
            x, h = resid_norm(rows, x, final_g, gate=m[:, 5], h_dtypes=(F32,), **f)
    y = h
    stack = lambda pairs, k, dt: jnp.stack([p[k] for p in pairs]).astype(dt)
    return (y[:t_p].reshape(b_p, L, D), y[t_p:].reshape(b_s, 1, D),
            stack(shifts, 0, state_rwkv_shift.dtype), stack(shifts, 1, state_rwkv_shift.dtype),
            stack(wkvs, 0, state_rwkv_wkv.dtype), stack(wkvs, 1, state_rwkv_wkv.dtype),
            stack(rets, 0, state_ret.dtype), stack(rets, 1, state_ret.dtype),
            stack(convs, 0, state_gdn_conv.dtype), stack(convs, 1, state_gdn_conv.dtype),
            stack(gdns, 0, state_gdn.dtype), stack(gdns, 1, state_gdn.dtype))
```

```python
import functools
import math

import jax
import jax.numpy as jnp
from jax import lax
from jax.experimental import pallas as pl
from jax.experimental.pallas import tpu as pltpu

F32 = jnp.float32
BF16 = jnp.bfloat16

PAST_LEN = 16384
TOP_K = 2
NORM_EPS = 1e-6
A_GN_EPS = 64e-5
B_GN_EPS = 1e-6
ROPE_BASE = 10000.0

LANES = 128
SUBLANES = 8
VMEM_LIMIT = 56 * 2 ** 20
VMEM_BUDGET = 46 * 2 ** 20


def _divisor_tile(n, target, mult):
    best = None
    for t in range(mult, min(n, target) + 1, mult):
        if n % t == 0:
            best = t
    return best if best is not None else n


def _params(sem, **kw):
    return pltpu.CompilerParams(dimension_semantics=sem, vmem_limit_bytes=VMEM_LIMIT, **kw)


def _mm_kernel(a_ref, w_ref, o_ref, *scratch, nk, tk):
    if nk == 1:
        o_ref[...] = jnp.dot(a_ref[...].astype(BF16), w_ref[...].astype(BF16),
                             preferred_element_type=F32).astype(o_ref.dtype)
        return
    acc_ref, = scratch
    k = pl.program_id(2)
    a = a_ref[:, pl.ds(pl.multiple_of(k * tk, LANES), tk)]
    part = jnp.dot(a.astype(BF16), w_ref[...].astype(BF16), preferred_element_type=F32)

    @pl.when(k == 0)
    def _():
        acc_ref[...] = part

    @pl.when(k > 0)
    def _():
        acc_ref[...] += part

    @pl.when(k == nk - 1)
    def _():
        o_ref[...] = acc_ref[...].astype(o_ref.dtype)


def _mm_tiles(M, K, N, a_bytes, n_w, tm_fixed=None):
    tk = K if K <= 4096 else _divisor_tile(K, 6144, LANES)
    for tm_t, tn_t in ((1664, 512), (1040, 512), (832, 512), (640, 512), (1040, 256), (640, 256), (416, 256),
                       (208, 256), (104, 128), (8, 128)):
        tm = tm_fixed or _divisor_tile(M, tm_t, SUBLANES)
        if N % LANES == 0:
            tn = _divisor_tile(N, tn_t, LANES)
        else:
            tn = N if N <= tn_t else tn_t
        need = (2 * tm * K * a_bytes + n_w * (2 * tk * tn * 4 + tk * tn * 2) + tm * tk * 2 * (a_bytes == 4)
                + tm * tn * 4 * (2 + n_w + 1))
        if need <= VMEM_BUDGET:
            return tm, tn, tk
    return tm, tn, tk


def _w_spec(widx, tk, tn, imap):
    return pl.BlockSpec((None,) * len(widx) + (tk, tn), lambda *g: tuple(widx) + imap(*g))


def mm(a, w, widx=(), out_dtype=F32):
    M, K = a.shape
    N = w.shape[-1]
    tm, tn, tk = _mm_tiles(M, K, N, a.dtype.itemsize, 1)
    nk = K // tk
    grid = (M // tm, pl.cdiv(N, tn), nk)
    return pl.pallas_call(
        functools.partial(_mm_kernel, nk=nk, tk=tk),
        out_shape=jax.ShapeDtypeStruct((M, N), out_dtype),
        grid=grid,
        in_specs=[pl.BlockSpec((tm, K), lambda i, j, k: (i, 0)),
                  _w_spec(widx, tk, tn, lambda i, j, k: (k, j))],
        out_specs=pl.BlockSpec((tm, tn), lambda i, j, k: (i, j)),
        scratch_shapes=[pltpu.VMEM((tm, tn), F32)] if nk > 1 else [],
        compiler_params=_params(("parallel", "parallel", "arbitrary")),
        name="mm",
    )(a, w)


def _swiglu_up_kernel(a_ref, wg_ref, wu_ref, o_ref):
    a = a_ref[...].astype(BF16)
    g = jnp.dot(a, wg_ref[...].astype(BF16), preferred_element_type=F32)
    u = jnp.dot(a, wu_ref[...].astype(BF16), preferred_element_type=F32)
    o_ref[...] = (g * jax.nn.sigmoid(g) * u).astype(o_ref.dtype)


def swiglu_up(a, wg, wu, widx=()):
    M, K = a.shape
    F = wg.shape[-1]
    tm, tn, _ = _mm_tiles(M, K, F, a.dtype.itemsize, 2)
    return pl.pallas_call(
        _swiglu_up_kernel,
        out_shape=jax.ShapeDtypeStruct((M, F), BF16),
        grid=(M // tm, F // tn),
        in_specs=[pl.BlockSpec((tm, K), lambda i, j: (i, 0)),
                  _w_spec(widx, K, tn, lambda i, j: (0, j)),
                  _w_spec(widx, K, tn, lambda i, j: (0, j))],
        out_specs=pl.BlockSpec((tm, tn), lambda i, j: (i, j)),
        compiler_params=_params(("parallel", "parallel")),
        name="swiglu_up",
    )(a, wg, wu)


class _Rows:
    def __init__(self, b_p, L, b_s):
        self.b_p, self.L, self.b_s = b_p, L, b_s
        self.t_p = b_p * L
        self.T = self.t_p + b_s
        self.tr = min(LANES, math.gcd(L, b_s))
        assert self.tr % SUBLANES == 0
        self.n_pt = self.t_p // self.tr
        self.tiles_per_seq = L // self.tr
        self.n_tiles = self.T // self.tr

    def row_spec(self, D):
        return pl.BlockSpec((self.tr, D), lambda i: (i, 0))

    def mod_specs(self, D):
        n_pt, tps, b_p = self.n_pt, self.tiles_per_seq, self.b_p
        return [pl.BlockSpec((1, 1, D), lambda i: (jnp.minimum(i // tps, b_p - 1), 0, 0)),
                pl.BlockSpec((self.tr, D), lambda i: (jnp.maximum(i - n_pt, 0), 0))]

    def split_mod(self, m):
        return m[:self.b_p, None, :], m[self.b_p:]


def _resid_norm_kernel(*refs, n_pt, has_resid, has_mod, h_dtypes, n_mix, tiles_per_seq):
    it = iter(refs)
    x_ref = next(it)
    is_p = pl.program_id(0) < n_pt

    def mod():
        p_ref, s_ref = next(it), next(it)
        return jnp.where(is_p, p_ref[0], s_ref[...])

    x = x_ref[...]
    if has_resid == 1:
        gate = mod()
        x = x + gate * next(it)[...]
    elif has_resid == 2:
        gate = mod()
        y1_ref, y2_ref, p1_ref, p2_ref = next(it), next(it), next(it), next(it)
        x = x + gate * (p1_ref[...] * y1_ref[...] + p2_ref[...] * y2_ref[...])
    g_ref = next(it)
    if has_mod:
        sc, sh = mod(), mod()
    if n_mix:
        mu_ref, shift_ref = next(it), next(it)
    if has_resid:
        next(it)[...] = x
    h = x * lax.rsqrt(jnp.mean(jnp.square(x), axis=-1, keepdims=True) + NORM_EPS) * g_ref[...]
    if has_mod:
        h = h * (1 + sc) + sh
    for dt in h_dtypes:
        next(it)[...] = h.astype(dt)
    if n_mix:
        mix_refs = [next(it) for _ in range(n_mix)]
        carry_ref = next(it)
        tr = h.shape[0]

        @pl.when(pl.program_id(0) == 0)
        def _():
            carry_ref[...] = jnp.zeros_like(carry_ref)

        carry = jnp.where(pl.program_id(0) % tiles_per_seq == 0, 0.0, carry_ref[...])
        first_row = lax.broadcasted_iota(jnp.int32, h.shape, 0) == 0
        prev = jnp.where(is_p, jnp.where(first_row, carry, pltpu.roll(h, 1, 0)), shift_ref[...])
        carry_ref[...] = h[tr - 1:tr, :]
        d = prev - h
        for s, m_ref in enumerate(mix_refs):
            m_ref[...] = (h + d * mu_ref[s:s + 1, :]).astype(m_ref.dtype)


def resid_norm(rows, x, g, y=None, gate=None, scale=None, shift=None, h_dtypes=(BF16,), pair=None, mix=None):
    T, D = x.shape
    has_resid, has_mod = 2 if pair is not None else int(y is not None), scale is not None
    args, specs = [x], [rows.row_spec(D)]
    if has_resid == 1:
        args += [*rows.split_mod(gate), y]
        specs += [*rows.mod_specs(D), rows.row_spec(D)]
    elif has_resid == 2:
        y2, p1, p2 = pair
        n_t, tr = rows.n_tiles, rows.tr
        col = pl.BlockSpec((tr, 1), lambda i: (i, 0))
        args += [*rows.split_mod(gate), y2, y2, p1, p2]
        specs += [*rows.mod_specs(D), rows.row_spec(D), pl.BlockSpec((tr, D), lambda i: (i + n_t, 0)), col, col]
    args.append(g.reshape(1, D))
    specs.append(pl.BlockSpec((1, D), lambda i: (0, 0)))
    if has_mod:
        args += [*rows.split_mod(scale), *rows.split_mod(shift)]
        specs += [*rows.mod_specs(D), *rows.mod_specs(D)]
    n_mix = 0
    if mix is not None:
        mu, prev_s = mix
        n_mix, n_pt = mu.shape[0], rows.n_pt
        args += [mu.astype(F32), prev_s.astype(F32)]
        specs += [pl.BlockSpec((n_mix, D), lambda i: (0, 0)),
                  pl.BlockSpec((rows.tr, D), lambda i: (jnp.maximum(i - n_pt, 0), 0))]
    out_shape, out_specs = [], []
    if has_resid:
        out_shape.append(jax.ShapeDtypeStruct((T, D), F32))
        out_specs.append(rows.row_spec(D))
    for dt in tuple(h_dtypes) + (BF16,) * n_mix:
        out_shape.append(jax.ShapeDtypeStruct((T, D), dt))
        out_specs.append(rows.row_spec(D))
    return pl.pallas_call(
        functools.partial(_resid_norm_kernel, n_pt=rows.n_pt, has_resid=has_resid, has_mod=has_mod,
                          h_dtypes=tuple(h_dtypes), n_mix=n_mix, tiles_per_seq=rows.tiles_per_seq),
        out_shape=out_shape, grid=(rows.n_tiles,), in_specs=specs, out_specs=out_specs,
        scratch_shapes=[pltpu.VMEM((1, D), F32)] if n_mix else [],
        compiler_params=_params(("arbitrary",) if n_mix else ("parallel",)),
        name="resid_norm",
    )(*args)


def _ret_kernel(lg_ref, cc_ref, cr_ref, cos_ref, sin_ref, q_ref, k_ref, v_ref, g_ref, s0_ref, o_ref, s_ref,
                *, C, hb, dk, dv):
    @pl.when(pl.program_id(2) == 0)
    def _():
        s_ref[...] = s0_ref[...]

    row = lax.broadcasted_iota(jnp.int32, (C, C), 0)
    col = lax.broadcasted_iota(jnp.int32, (C, C), 1)
    causal = row >= col
    cos, sin = cos_ref[...], sin_ref[...]
    half = dk // 2

    def rotary(x):
        x1, x2 = x[:, :half], x[:, half:]
        return jnp.concatenate([x1 * cos - x2 * sin, x1 * sin + x2 * cos], axis=-1)

    for h in range(hb):
        lg = lg_ref[h, 0:1, 0:1]
        gc = lg * cc_ref[...]
        gr = lg * cr_ref[...]
        dec = jnp.where(causal, jnp.exp(jnp.where(causal, gc - gr, 0.0)), 0.0)
        q = rotary(q_ref[:, h * dk:(h + 1) * dk])
        k = rotary(k_ref[:, h * dk:(h + 1) * dk]) * dk ** -0.5
        v = v_ref[:, h * dv:(h + 1) * dv]
        S = s_ref[0, h]
        o = _bdot(_bdot(q, k, ((1,), (1,))) * dec, v) + _bdot(q * jnp.exp(gc), S)
        g_end = lg * cc_ref[C - 1:C, :]
        s_ref[0, h] = S * jnp.exp(g_end) + _bdot(k * jnp.exp(g_end - gc), v, ((0,), (0,)))
        mu = jnp.mean(o, axis=-1, keepdims=True)
        var = jnp.mean(jnp.square(o - mu), axis=-1, keepdims=True)
        gate = g_ref[:, h * dv:(h + 1) * dv]
        o_ref[:, h * dv:(h + 1) * dv] = ((gate * jax.nn.sigmoid(gate)) * ((o - mu) * lax.rsqrt(var + B_GN_EPS))
                                         ).astype(o_ref.dtype)


def retention_chunks(proj, s0, lg, cnt, pos, B, L, H, dk, dv, C):
    assert (2 * dk) % dv == 0
    n = L // C
    hb = _divisor_tile(H, 4, 1)
    nb = H // hb
    lg_t = jnp.broadcast_to(lg.astype(F32)[:, None, None], (H, SUBLANES, LANES))
    inv = ROPE_BASE ** (-jnp.arange(0, dk, 2, dtype=F32) / dk)
    ang = pos.astype(F32)[:, None] * inv[None, :]
    v_sec = (2 * dk) // dv
    qk = lambda sec: pl.BlockSpec((C, hb * dk), lambda b, h, c: (b * n + c, sec * nb + h))
    vg = lambda sec: pl.BlockSpec((C, hb * dv), lambda b, h, c: (b * n + c, sec * nb + h))
    trig = pl.BlockSpec((C, dk // 2), lambda b, h, c: (c, 0))
    st = pl.BlockSpec((1, hb, dk, dv), lambda b, h, c: (b, h, 0, 0))
    return pl.pallas_call(
        functools.partial(_ret_kernel, C=C, hb=hb, dk=dk, dv=dv),
        out_shape=[jax.ShapeDtypeStruct((B * L, H * dv), BF16), jax.ShapeDtypeStruct((B, H, dk, dv), F32)],
        grid=(B, nb, n),
        in_specs=[pl.BlockSpec((hb, SUBLANES, LANES), lambda b, h, c: (h, 0, 0)),
                  pl.BlockSpec((C, 1), lambda b, h, c: (0, 0)),
                  pl.BlockSpec((1, C), lambda b, h, c: (0, 0)),
                  trig, trig, qk(0), qk(1), vg(v_sec), vg(v_sec + 1), st],
        out_specs=[pl.BlockSpec((C, hb * dv), lambda b, h, c: (b * n + c, h)), st],
        compiler_params=_params(("parallel", "parallel", "arbitrary")),
        name="retention",
    )(lg_t, cnt.reshape(C, 1), cnt.reshape(1, C), jnp.cos(ang), jnp.sin(ang), proj, proj, proj, proj, s0)


def _bdot(a, b, dims=((1,), (0,))):
    return lax.dot_general(a.astype(BF16), b.astype(BF16), (dims, ((), ())), preferred_element_type=F32)


def _unit_lower_inverses(ms, C):
    row = lax.broadcasted_iota(jnp.int32, (C, C), 0)
    col = lax.broadcasted_iota(jnp.int32, (C, C), 1)
    eye = jnp.where(row == col, 1.0, 0.0)
    ts = [eye - jnp.where((row >> 1) == (col >> 1), m, 0.0) for m in ms]
    sh = 1
    while (1 << sh) < C:
        same_big = (row >> (sh + 1)) == (col >> (sh + 1))
        same_small = (row >> sh) == (col >> sh)
        tl = [_bdot(t, jnp.where(same_big, jnp.where(same_small, 0.0, m), 0.0)) for t, m in zip(ts, ms)]
        ts = [t - _bdot(x, t) for t, x in zip(ts, tl)]
        sh += 1
    return ts


def _gdn_kernel(q_ref, k_ref, v_ref, gc_ref, gr_ref, b_ref, z_ref, ng_ref, s0_ref, o_ref, s_ref, *, C, nc):
    @pl.when(pl.program_id(2) == 0)
    def _():
        s_ref[...] = s0_ref[...]

    row = lax.broadcasted_iota(jnp.int32, (C, C), 0)
    col = lax.broadcasted_iota(jnp.int32, (C, C), 1)
    cs = range(nc)
    rows = [slice(c * C, (c + 1) * C) for c in cs]
    q = [q_ref[r, :] for r in rows]
    k = [k_ref[r, :] for r in rows]
    gc = [gc_ref[0, 0, r, :] for r in rows]
    beta = [b_ref[0, 0, r, :] for r in rows]
    dec = [jnp.exp(jnp.where(row >= col, gc[c] - gr_ref[0, 0, c], -jnp.inf)) for c in cs]
    kb = [k[c] * beta[c] for c in cs]
    ms = [jnp.where(row > col, _bdot(kb[c], k[c], ((1,), (1,))) * dec[c], 0.0) for c in cs]
    attn = [_bdot(q[c], k[c], ((1,), (1,))) * dec[c] for c in cs]
    ts = _unit_lower_inverses(ms, C)
    u = [_bdot(ts[c], v_ref[rows[c], :] * beta[c]) for c in cs]
    w = [_bdot(ts[c], kb[c] * jnp.exp(gc[c])) for c in cs]
    g_end = [gc[c][C - 1:C, :] for c in cs]
    kd = [k[c] * jnp.exp(g_end[c] - gc[c]) for c in cs]
    lhs = [jnp.concatenate([q[c] * jnp.exp(gc[c]) - _bdot(attn[c], w[c]), _bdot(kd[c], w[c], ((0,), (0,)))],
                           axis=0).astype(BF16) for c in cs]
    o_u = [_bdot(attn[c], u[c]) for c in cs]
    s_u = [_bdot(kd[c], u[c], ((0,), (0,))) for c in cs]
    S = s_ref[0, 0]
    for c in cs:
        ls = _bdot(lhs[c], S)
        o = ls[:C] + o_u[c]
        z = z_ref[rows[c], :]
        o = o * lax.rsqrt(jnp.mean(jnp.square(o), axis=-1, keepdims=True) + NORM_EPS) * ng_ref[...]
        o_ref[rows[c], :] = (o * (z * jax.nn.sigmoid(z))).astype(o_ref.dtype)
        S = S * jnp.exp(g_end[c]) - ls[C:] + s_u[c]
    s_ref[0, 0] = S


def gdn_chunks(q, k, v, g, beta, proj, z_col, norm_g, s0, B, L, H, d):
    C = _divisor_tile(L, 64, SUBLANES)
    n = L // C
    nc = _divisor_tile(n, 16, 1)
    G = jnp.cumsum(g.reshape(B, n, C, H), axis=2).transpose(0, 3, 1, 2)
    tok = pl.BlockSpec((nc * C, d), lambda b, h, c: (b * (n // nc) + c, h))
    z_tok = pl.BlockSpec((nc * C, d), lambda b, h, c: (b * (n // nc) + c, z_col // d + h))
    colv = pl.BlockSpec((1, 1, nc * C, 1), lambda b, h, c: (b, h, c, 0))
    st = pl.BlockSpec((1, 1, d, d), lambda b, h, c: (b, h, 0, 0))
    return pl.pallas_call(
        functools.partial(_gdn_kernel, C=C, nc=nc),
        out_shape=[jax.ShapeDtypeStruct((B * L, H * d), BF16), jax.ShapeDtypeStruct((B, H, d, d), F32)],
        grid=(B, H, n // nc),
        in_specs=[tok, tok, tok, colv, pl.BlockSpec((1, 1, nc, 1, C), lambda b, h, c: (b, h, c, 0, 0)), colv,
                  z_tok, pl.BlockSpec((1, d), lambda b, h, c: (0, 0)), st],
        out_specs=[tok, st],
        compiler_params=_params(("parallel", "parallel", "arbitrary")),
        name="gdn_chunks",
    )(q, k, v, G.reshape(B, H, L, 1), G.reshape(B, H, n, 1, C),
      beta.reshape(B, L, H).transpose(0, 2, 1).reshape(B, H, L, 1), proj, norm_g.reshape(1, d).astype(F32), s0)


def _gdn_step_kernel(q_ref, k_ref, v_ref, al_ref, be_ref, s0_ref, o_ref, s_ref, *, bs, hb, d):
    half = SUBLANES // 2
    top = lax.broadcasted_iota(jnp.int32, (SUBLANES, d), 0) < half
    row0 = lax.broadcasted_iota(jnp.int32, (SUBLANES, d), 0) == 0
    hs = range(hb)
    sls = [slice(h * d, (h + 1) * d) for h in hs]
    rep = lambda x: jnp.broadcast_to(x, (SUBLANES, d))
    for b in range(bs):
        q = [q_ref[b, :, sl] for sl in sls]
        k = [k_ref[b, :, sl] for sl in sls]
        alpha = [al_ref[b, :, sl] for sl in sls]
        S = [s0_ref[b, h] for h in hs]
        ks_qs = [_bdot(jnp.where(top, rep(k[h]), rep(q[h])), S[h]) for h in hs]
        v_new = [be_ref[b, :, sls[h]] * (v_ref[b, :, sls[h]] - alpha[h] * ks_qs[h][0:1, :]) for h in hs]
        kv = [_bdot(jnp.where(row0, rep(k[h]), 0.0), rep(v_new[h]), ((0,), (0,))) for h in hs]
        for h in hs:
            s_ref[b, h] = S[h] * alpha[h] + kv[h]
        o_ref[b] = jnp.concatenate([alpha[h] * ks_qs[h][half:half + 1, :]
                                    + jnp.sum(q[h] * k[h], axis=-1, keepdims=True) * v_new[h] for h in hs], axis=-1)


def gdn_step(q, k, v, alpha, beta, s0, H, d):
    B = q.shape[0]
    hb, bs = _divisor_tile(H, 8, 1), _divisor_tile(B, 4, 1)
    lanes = lambda t: jnp.broadcast_to(t[:, :, None], (B, H, d)).reshape(B, 1, H * d)
    vec = pl.BlockSpec((bs, 1, hb * d), lambda b, h: (b, 0, h))
    st = pl.BlockSpec((bs, hb, d, d), lambda b, h: (b, h, 0, 0))
    o, s = pl.pallas_call(
        functools.partial(_gdn_step_kernel, bs=bs, hb=hb, d=d),
        out_shape=[jax.ShapeDtypeStruct((B, 1, H * d), F32), jax.ShapeDtypeStruct((B, H, d, d), F32)],
        grid=(B // bs, H // hb),
        in_specs=[vec] * 5 + [st],
        out_specs=[vec, st],
        compiler_params=_params(("parallel", "parallel")),
        name="gdn_step",
    )(q[:, None, :], k[:, None, :], v[:, None, :], lanes(alpha), lanes(beta), s0)
    return o[:, 0], s


def _rwkv_step_kernel(r_ref, w_ref, k_ref, v_ref, a_ref, b_ref, s0_ref, y_ref, s_ref, *, bs, hb, N):
    rid = lax.broadcasted_iota(jnp.int32, (SUBLANES, N), 0)
    nt = ((1,), (1,))
    hs = range(hb)
    sls = [slice(h * N, (h + 1) * N) for h in hs]
    pieces = lambda x: (x.astype(BF16), (x - x.astype(BF16).astype(F32)).astype(BF16))
    for b in range(bs):
        rep = lambda ref, h: jnp.broadcast_to(ref[b, :, sls[h]], (SUBLANES, N))
        S = [s0_ref[b, h] for h in hs]
        sa = [_bdot(rep(a_ref, h), S[h], nt) for h in hs]
        left = [pieces(jnp.where(rid == 0, sa[h], jnp.where(rid == 1, rep(v_ref, h), 0.0))) for h in hs]
        right = [pieces(jnp.where(rid == 0, rep(b_ref, h), jnp.where(rid == 1, rep(k_ref, h), 0.0))) for h in hs]
        rank1 = [lax.dot_general(jnp.concatenate([left[h][0], left[h][0], left[h][1]], axis=0),
                                 jnp.concatenate([right[h][0], right[h][1], right[h][0]], axis=0),
                                 (((0,), (0,)), ((), ())), preferred_element_type=F32) for h in hs]
        S = [S[h] * w_ref[b, :, sls[h]] + rank1[h] for h in hs]
        y = [_bdot(rep(r_ref, h), S[h], nt)[0:1, :] for h in hs]
        for h in hs:
            s_ref[b, h] = S[h]
        y_ref[b] = jnp.concatenate(y, axis=-1)


def rwkv_step(r, w, k, v, a, b, s0, H, N):
    B = r.shape[0]
    hb, bs = _divisor_tile(H, 16, 1), _divisor_tile(B, 4, 1)
    vec = pl.BlockSpec((bs, 1, hb * N), lambda b_, h: (b_, 0, h))
    st = pl.BlockSpec((bs, hb, N, N), lambda b_, h: (b_, h, 0, 0))
    y, s = pl.pallas_call(
        functools.partial(_rwkv_step_kernel, bs=bs, hb=hb, N=N),
        out_shape=[jax.ShapeDtypeStruct((B, 1, H * N), F32), jax.ShapeDtypeStruct((B, H, N, N), F32)],
        grid=(B // bs, H // hb),
        in_specs=[vec] * 6 + [st],
        out_specs=[vec, st],
        compiler_params=_params(("parallel", "parallel")),
        name="rwkv_step",
    )(*(t[:, None, :] for t in (r, w, k, v, a, b)), s0)
    return y[:, 0], s


def _rwkv_kernel(r_ref, k_ref, v_ref, a_ref, b_ref, g_ref, gate_ref, lnw_ref, lnb_ref, rk_ref, s0_ref, y_ref, s_ref,
                 *, C, nc, N):
    @pl.when(pl.program_id(2) == 0)
    def _():
        s_ref[...] = s0_ref[...]

    n_h = r_ref.shape[1] // N
    row = lax.broadcasted_iota(jnp.int32, (C, C), 0)
    col = lax.broadcasted_iota(jnp.int32, (C, C), 1)
    strict, incl = row > col, row >= col
    ones_lower = jnp.where(incl, 1.0, 0.0).astype(BF16)
    rows = [slice(c * C, (c + 1) * C) for c in range(nc)]
    at, rt, bh, kh, bt, kt, e_end = [], [], [], [], [], [], []
    for rs in rows:
        g = g_ref[rs, :]
        g_hi = g.astype(BF16)
        rem = g - g_hi.astype(F32)
        g_mid = rem.astype(BF16)
        g_lo = (rem - g_mid.astype(F32)).astype(BF16)
        G = sum(jnp.dot(ones_lower, p, preferred_element_type=F32) for p in (g_hi, g_mid, g_lo))
        e_neg = jnp.exp(-G)
        to_end = jnp.exp(G[C - 1:C, :] - G)
        b, k = b_ref[rs, :], k_ref[rs, :]
        at.append(a_ref[rs, :] * jnp.exp(G - g))
        rt.append(r_ref[rs, :] * jnp.exp(G))
        bh.append(b * e_neg)
        kh.append(k * e_neg)
        bt.append(b * to_end)
        kt.append(k * to_end)
        e_end.append(jnp.exp(G[C - 1:C, :]))
    chains = [(c, h) for c in range(nc) for h in range(n_h)]
    hl = lambda x, h: x[:, h * N:(h + 1) * N]
    nt = ((1,), (1,))
    tn = ((0,), (0,))
    a_ab = [jnp.where(strict, _bdot(hl(at[c], h), hl(bh[c], h), nt), 0.0) for c, h in chains]
    a_ak = [jnp.where(strict, _bdot(hl(at[c], h), hl(kh[c], h), nt), 0.0) for c, h in chains]
    a_rb = [jnp.where(incl, _bdot(hl(rt[c], h), hl(bh[c], h), nt), 0.0) for c, h in chains]
    a_rk = [jnp.where(incl, _bdot(hl(rt[c], h), hl(kh[c], h), nt), 0.0) for c, h in chains]
    v = [hl(v_ref[rows[c], :], h) for c, h in chains]
    ts = _unit_lower_inverses([-m for m in a_ab], C)
    t_a = [_bdot(ts[i], hl(at[c], h)) for i, (c, h) in enumerate(chains)]
    akv = [_bdot(a_ak[i], v[i]) for i in range(len(chains))]
    uv = [_bdot(ts[i], akv[i]) for i in range(len(chains))]
    lhs_y = [hl(rt[c], h) + _bdot(a_rb[i], t_a[i]) for i, (c, h) in enumerate(chains)]
    m2 = [_bdot(t_a[i], hl(bt[c], h), tn) for i, (c, h) in enumerate(chains)]
    y_c = [_bdot(a_rb[i], uv[i]) + _bdot(a_rk[i], v[i]) for i in range(len(chains))]
    s_c = [_bdot(uv[i], hl(bt[c], h), tn) + _bdot(v[i], hl(kt[c], h), tn) for i, (c, h) in enumerate(chains)]
    S = [s_ref[0, h] for h in range(n_h)]
    lanes = r_ref.shape[1]
    li = lax.broadcasted_iota(jnp.int32, (lanes, lanes), 0) // N
    lj = lax.broadcasted_iota(jnp.int32, (lanes, lanes), 1) // N
    same_head = jnp.where(li == lj, 1.0, 0.0).astype(BF16)

    def head_sum(x):
        hi = x.astype(BF16)
        lo = (x - hi.astype(F32)).astype(BF16)
        return (jnp.dot(hi, same_head, preferred_element_type=F32)
                + jnp.dot(lo, same_head, preferred_element_type=F32))

    ys = []
    for i, (c, h) in enumerate(chains):
        ys.append(_bdot(lhs_y[i], S[h], nt) + y_c[i])
        S[h] = S[h] * hl(e_end[c], h) + _bdot(S[h], m2[i]) + s_c[i]
    for h in range(n_h):
        s_ref[0, h] = S[h]
    y = [jnp.concatenate(ys[c * n_h:(c + 1) * n_h], axis=-1) for c in range(nc)]
    d = [y[c] - head_sum(y[c]) * (1.0 / N) for c in range(nc)]
    var = [head_sum(d[c] * d[c]) * (1.0 / N) for c in range(nc)]
    rkv = [head_sum(r_ref[rs, :] * k_ref[rs, :] * rk_ref[...]) * v_ref[rs, :] for rs in rows]
    for c, rs in enumerate(rows):
        yn = d[c] * lax.rsqrt(var[c] + A_GN_EPS) * lnw_ref[...] + lnb_ref[...]
        y_ref[rs, :] = ((yn + rkv[c]) * gate_ref[rs, :]).astype(y_ref.dtype)


def rwkv_chunks(r, k, v, a, b, g, gate, ln_w, ln_b, r_k, s0, B, L, H, N):
    C = _divisor_tile(L, 64, SUBLANES)
    n = L // C
    nc = _divisor_tile(n, 8, 1)
    n_h = LANES // N
    tok = pl.BlockSpec((nc * C, LANES), lambda b_, p, c: (b_ * (n // nc) + c, p))
    vec = pl.BlockSpec((1, LANES), lambda b_, p, c: (0, p))
    st = pl.BlockSpec((1, n_h, N, N), lambda b_, p, c: (b_, p, 0, 0))
    row = lambda t: t.reshape(1, H * N).astype(F32)
    return pl.pallas_call(
        functools.partial(_rwkv_kernel, C=C, nc=nc, N=N),
        out_shape=[jax.ShapeDtypeStruct((B * L, H * N), BF16), jax.ShapeDtypeStruct((B, H, N, N), F32)],
        grid=(B, H // n_h, n // nc),
        in_specs=[tok] * 7 + [vec] * 3 + [st],
        out_specs=[tok, st],
        compiler_params=_params(("parallel", "parallel", "arbitrary")),
        name="rwkv_chunks",
    )(r, k, v, a, b, g, gate, row(ln_w), row(ln_b), row(r_k), s0)


def _head_norm(y, eps):
    mu = jnp.mean(y, axis=-1, keepdims=True)
    var = jnp.mean(jnp.square(y - mu), axis=-1, keepdims=True)
    return (y - mu) * lax.rsqrt(var + eps)


def _rwkv(rows, h, mixes, s0, v_first, P, j):
    b_p, L, b_s, t_p = rows.b_p, rows.L, rows.b_s, rows.t_p
    D = h.shape[1]
    H, N = s0.shape[1], s0.shape[2]
    new_shift_p, new_shift_s = h[:t_p].reshape(b_p, L, D)[:, -1], h[t_p:]
    xr, xk, xv, xw, xa, xg = mixes
    r = mm(xr, P['rwkv_w_rkv'], (j, 0))
    k = mm(xk, P['rwkv_w_rkv'], (j, 1))
    v = mm(xv, P['rwkv_w_rkv'], (j, 2))
    w_log = -jax.nn.softplus(-(P['rwkv_w0'][j] + mm(jnp.tanh(mm(xw, P['rwkv_w1'], (j,))), P['rwkv_w2'], (j,)))) - 0.5
    log_w = -jnp.exp(w_log)
    if v_first is None:
        v_first = v
    else:
        mix = jax.nn.sigmoid(P['rwkv_v0'][j - 1] + mm(mm(xv, P['rwkv_v1'], (j - 1,)), P['rwkv_v2'], (j - 1,)))
        v = v + (v_first - v) * mix
    a = jax.nn.sigmoid(P['rwkv_a0'][j] + mm(mm(xa, P['rwkv_a1'], (j,)), P['rwkv_a2'], (j,)))
    g = mm(jax.nn.sigmoid(mm(xg, P['rwkv_g1'], (j,))), P['rwkv_g2'], (j,))
    T = rows.T
    kk = (k * P['rwkv_k_k'][j]).reshape(T, H, N)
    kk = (kk / jnp.maximum(jnp.sqrt(jnp.sum(jnp.square(kk), axis=-1, keepdims=True)), 1e-12)).reshape(T, D)
    k = k * (1 + (a - 1) * P['rwkv_k_a'][j])
    na, nb = -kk, kk * a
    ln_w, ln_b, r_k = P['rwkv_ln_w'][j], P['rwkv_ln_b'][j], P['rwkv_r_k'][j]
    y_p, s_p = rwkv_chunks(r, k, v, na, nb, log_w, g, ln_w, ln_b, r_k, jnp.zeros((b_p,) + s0.shape[1:], F32),
                           b_p, L, H, N)
    rs, ks, vs = r[t_p:], k[t_p:], v[t_p:]
    y_s, s_s = rwkv_step(rs, jnp.exp(log_w[t_p:]), ks, vs, na[t_p:], nb[t_p:], s0.astype(F32), H, N)
    y_s = _head_norm(y_s.reshape(b_s, H, N), A_GN_EPS).reshape(b_s, D) * ln_w + ln_b
    bonus = jnp.sum((rs * ks).reshape(b_s, H, N) * r_k, axis=-1, keepdims=True) * vs.reshape(b_s, H, N)
    y_s = ((y_s + bonus.reshape(b_s, D)) * g[t_p:]).astype(BF16)
    out = mm(jnp.concatenate([y_p, y_s], axis=0), P['rwkv_w_o'], (j,))
    return out, (new_shift_p, new_shift_s), (s_p, s_s), v_first


def _retention(rows, h, s0, P, j):
    b_p, L, b_s, t_p, T = rows.b_p, rows.L, rows.b_s, rows.t_p, rows.T
    D = h.shape[1]
    H, dk, dv = s0.shape[1], s0.shape[2], s0.shape[3]
    proj = mm(h, P['ret_w_in'], (j,))
    lg = jnp.log1p(-jnp.exp2(-5.0 - jnp.arange(H, dtype=F32)))
    C = _divisor_tile(L, 256, SUBLANES)
    o_p, s_p = retention_chunks(proj, jnp.zeros((b_p,) + s0.shape[1:], F32), lg, jnp.arange(1, C + 1, dtype=F32),
                                jnp.arange(L), b_p, L, H, dk, dv, C)
    proj_s = jnp.pad(proj[t_p:, None, :], ((0, 0), (0, SUBLANES - 1), (0, 0))).reshape(b_s * SUBLANES, -1)
    o_s, s_s = retention_chunks(proj_s, s0.astype(F32), lg, jnp.ones((SUBLANES,), F32),
                                jnp.full((SUBLANES,), PAST_LEN), b_s, SUBLANES, H, dk, dv, SUBLANES)
    o = jnp.concatenate([o_p, o_s.reshape(b_s, SUBLANES, H * dv)[:, 0]], axis=0)
    out = mm(o, P['ret_w_o'], (j,))
    return out, (s_p, s_s)


def _gdn(rows, h, conv_buf, s0, P, j):
    b_p, L, b_s, t_p, T = rows.b_p, rows.L, rows.b_s, rows.t_p, rows.T
    D = h.shape[1]
    H, dh = s0.shape[1], s0.shape[2]
    n_conv = conv_buf.shape[1] + 1
    proj = mm(h, P['gdn_w_in'], (j,))
    qkv, z, a, b = jnp.split(proj, [3 * D, 4 * D, 4 * D + H], axis=-1)
    cat_p = jnp.concatenate([jnp.zeros((b_p, n_conv - 1, 3 * D), F32), qkv[:t_p].reshape(b_p, L, 3 * D)], axis=1)
    cat_s = jnp.concatenate([conv_buf.astype(F32), qkv[t_p:, None, :]], axis=1)
    new_buf = (cat_p[:, -(n_conv - 1):], cat_s[:, -(n_conv - 1):])
    cw = P['gdn_conv_w'][j]
    conv = lambda cat, n: sum(cat[:, i:i + n] * cw[i] for i in range(n_conv))
    qkv = jnp.concatenate([conv(cat_p, L).reshape(t_p, 3 * D), conv(cat_s, 1).reshape(b_s, 3 * D)], axis=0)
    qkv = jax.nn.silu(qkv)
    q, k, v = (t.reshape(T, H, dh) for t in jnp.split(qkv, 3, axis=-1))
    l2 = lambda x: x * lax.rsqrt(jnp.sum(jnp.square(x), axis=-1, keepdims=True) + 1e-6)
    q = l2(q) * dh ** -0.5
    k = l2(k)
    beta = jax.nn.sigmoid(b)
    log_alpha = -jnp.exp(P['gdn_a_log'][j].astype(F32)) * jax.nn.softplus(a + P['gdn_dt_bias'][j])
    o_p, s_p = gdn_chunks(q.reshape(T, D), k.reshape(T, D), v.reshape(T, D), log_alpha[:t_p], beta[:t_p],
                          proj, 3 * D, P['gdn_norm_g'][j], jnp.zeros((b_p,) + s0.shape[1:], F32), b_p, L, H, dh)
    o_s, s_s = gdn_step(q[t_p:].reshape(b_s, D), k[t_p:].reshape(b_s, D), v[t_p:].reshape(b_s, D),
                        jnp.exp(log_alpha[t_p:]), beta[t_p:], s0.astype(F32), H, dh)
    o_s = o_s.reshape(b_s, H, dh)
    o_s = o_s * lax.rsqrt(jnp.mean(jnp.square(o_s), axis=-1, keepdims=True) + NORM_EPS) * P['gdn_norm_g'][j]
    o_s = (o_s * jax.nn.silu(z[t_p:].reshape(b_s, H, dh))).reshape(b_s, D).astype(BF16)
    out = mm(jnp.concatenate([o_p, o_s], axis=0), P['gdn_w_o'], (j,))
    return out, new_buf, (s_p, s_s)


def _router_kernel(l_ref, b_ref, i1_ref, i2_ref, p1_ref, p2_ref, *, n_e):
    logit = l_ref[...]
    s = logit + b_ref[...]
    lane = lax.broadcasted_iota(jnp.int32, s.shape, 1)
    first = lambda v: jnp.min(jnp.where(v == jnp.max(v, axis=-1, keepdims=True), lane, n_e), axis=-1, keepdims=True)
    i1 = first(s)
    i2 = first(jnp.where(lane == i1, -jnp.inf, s))
    pick = lambda i: jnp.sum(jnp.where(lane == i, logit, 0.0), axis=-1, keepdims=True)
    l1, l2 = pick(i1), pick(i2)
    m = jnp.maximum(l1, l2)
    e1, e2 = jnp.exp(l1 - m), jnp.exp(l2 - m)
    i1_ref[...] = i1
    i2_ref[...] = i2
    p1_ref[...] = e1 / (e1 + e2)
    p2_ref[...] = e2 / (e1 + e2)


def router_top2(logits, bias):
    T, n_e = logits.shape
    tr = _divisor_tile(T, 1040, SUBLANES)
    col = lambda: pl.BlockSpec((tr, 1), lambda i: (i, 0))
    return pl.pallas_call(
        functools.partial(_router_kernel, n_e=n_e),
        out_shape=[jax.ShapeDtypeStruct((T, 1), jnp.int32)] * 2 + [jax.ShapeDtypeStruct((T, 1), F32)] * 2,
        grid=(T // tr,),
        in_specs=[pl.BlockSpec((tr, n_e), lambda i: (i, 0)), pl.BlockSpec((1, n_e), lambda i: (0, 0))],
        out_specs=[col(), col(), col(), col()],
        compiler_params=_params(("parallel",)),
        name="router_top2",
    )(logits, bias.reshape(1, n_e).astype(F32))


def _gather_kernel(idx_ref, src_ref, o_ref, *scratch, tg):
    sem = scratch[-1]
    dst_ref = scratch[0] if len(scratch) == 2 else o_ref
    base = pl.program_id(0) * tg
    row_copy = lambda src_row, r: pltpu.make_async_copy(src_ref.at[pl.ds(src_row, 1)], dst_ref.at[pl.ds(r, 1)], sem)

    def start(r, c):
        row_copy(idx_ref[base + r], r).start()
        return c

    def wait(r, c):
        row_copy(0, r).wait()
        return c

    lax.fori_loop(0, tg, start, 0)
    lax.fori_loop(0, tg, wait, 0)
    if dst_ref is not o_ref:
        o_ref[...] = dst_ref[...].astype(o_ref.dtype)


def gather_rows(src, idx, tg, out_dtype=None):
    n_out = idx.shape[0]
    D = src.shape[1]
    out_dtype = out_dtype or src.dtype
    stage = [pltpu.VMEM((tg, D), src.dtype)] if out_dtype != src.dtype else []
    return pl.pallas_call(
        functools.partial(_gather_kernel, tg=tg),
        out_shape=jax.ShapeDtypeStruct((n_out, D), out_dtype),
        grid_spec=pltpu.PrefetchScalarGridSpec(
            num_scalar_prefetch=1, grid=(n_out // tg,),
            in_specs=[pl.BlockSpec(memory_space=pl.ANY)],
            out_specs=pl.BlockSpec((tg, D), lambda i, idx: (i, 0)),
            scratch_shapes=stage + [pltpu.SemaphoreType.DMA(())]),
        compiler_params=_params(("arbitrary",), disable_bounds_checks=True),
        name="gather_rows",
    )(idx, src)


def _group_up_kernel(te_ref, ts_ref, ok_ref, a_ref, wg_ref, wu_ref, o_ref):
    ok = ok_ref[pl.program_id(0)] > 0

    @pl.when(ok)
    def _():
        _swiglu_up_kernel(a_ref, wg_ref, wu_ref, o_ref)

    @pl.when(jnp.logical_not(ok))
    def _():
        o_ref[...] = jnp.zeros_like(o_ref)


def _group_down_kernel(te_ref, ts_ref, ok_ref, a_ref, w_ref, o_ref, *scratch, nk, tk):
    ok = ok_ref[pl.program_id(0)] > 0

    @pl.when(ok)
    def _():
        _mm_kernel(a_ref, w_ref, o_ref, *scratch, nk=nk, tk=tk)

    @pl.when(jnp.logical_not(ok))
    def _():
        o_ref[...] = jnp.zeros_like(o_ref)


def grouped_swiglu_up(a, wg, wu, l, tiles, tm):
    M, K = a.shape
    F = wg.shape[-1]
    _, tn, _ = _mm_tiles(M, K, F, a.dtype.itemsize, 2, tm_fixed=tm)
    nj = F // tn
    w_spec = lambda: pl.BlockSpec((None, None, K, tn),
                                  lambda t, j, te, ts, ok: (l, te[t], 0, jnp.where(ok[t] > 0, j, nj - 1)))
    return pl.pallas_call(
        _group_up_kernel,
        out_shape=jax.ShapeDtypeStruct((M, F), BF16),
        grid_spec=pltpu.PrefetchScalarGridSpec(
            num_scalar_prefetch=3, grid=(M // tm, nj),
            in_specs=[pl.BlockSpec((tm, K), lambda t, j, te, ts, ok: (ts[t], 0)), w_spec(), w_spec()],
            out_specs=pl.BlockSpec((tm, tn), lambda t, j, te, ts, ok: (t, j))),
        compiler_params=_params(("arbitrary", "arbitrary")),
        name="grouped_swiglu_up",
    )(*tiles, a, wg, wu)


def grouped_mm(a, w, l, tiles, tm):
    M, K = a.shape
    N = w.shape[-1]
    _, tn, tk = _mm_tiles(M, K, N, a.dtype.itemsize, 1, tm_fixed=tm)
    nk, nj = K // tk, N // tn

    def w_map(t, j, k, te, ts, ok):
        live = ok[t] > 0
        return l, te[t], jnp.where(live, k, nk - 1), jnp.where(live, j, nj - 1)

    return pl.pallas_call(
        functools.partial(_group_down_kernel, nk=nk, tk=tk),
        out_shape=jax.ShapeDtypeStruct((M, N), F32),
        grid_spec=pltpu.PrefetchScalarGridSpec(
            num_scalar_prefetch=3, grid=(M // tm, nj, nk),
            in_specs=[pl.BlockSpec((tm, K), lambda t, j, k, te, ts, ok: (ts[t], 0)),
                      pl.BlockSpec((None, None, tk, tn), w_map)],
            out_specs=pl.BlockSpec((tm, tn), lambda t, j, k, te, ts, ok: (t, j)),
            scratch_shapes=[pltpu.VMEM((tm, tn), F32)] if nk > 1 else []),
        compiler_params=_params(("arbitrary", "arbitrary", "arbitrary")),
        name="grouped_mm",
    )(*tiles, a, w)


def _moe(rows, h, l, w_router, b_router, w_gate, w_up, w_down):
    T, D = h.shape
    n_e = w_router.shape[-1]
    i1, i2, p1, p2 = router_top2(mm(h, w_router, (l,)), b_router[l])
    tm = 768 if TOP_K * T >= 8 * 768 else 64
    n_tiles = pl.cdiv(TOP_K * T, tm) + n_e
    pair_e = jnp.concatenate([i1[:, 0], i2[:, 0]])
    onehot = (pair_e[:, None] == jnp.arange(n_e)[None, :]).astype(jnp.int32)
    rank = jnp.take_along_axis(jnp.cumsum(onehot, axis=0), pair_e[:, None], axis=1)[:, 0] - 1
    count = jnp.sum(onehot, axis=0)
    padded = (count + tm - 1) // tm * tm
    g_end = jnp.cumsum(padded)
    slot = ((g_end - padded)[pair_e] + rank).astype(jnp.int32)
    pair_tok = jnp.concatenate([jnp.arange(T, dtype=jnp.int32)] * TOP_K)
    src = jnp.zeros((n_tiles * tm,), jnp.int32).at[slot].set(pair_tok)
    n_used = g_end[-1] // tm
    t_id = jnp.minimum(jnp.arange(n_tiles), n_used - 1).astype(jnp.int32)
    tile_e = jnp.minimum(jnp.sum(t_id[:, None] * tm >= g_end[None, :], axis=1), n_e - 1).astype(jnp.int32)
    tile_ok = (jnp.arange(n_tiles) < n_used).astype(jnp.int32)
    tiles = (tile_e, t_id, tile_ok)
    hg = gather_rows(h, src, min(tm, 256), BF16)
    yg = grouped_mm(grouped_swiglu_up(hg, w_gate, w_up, l, tiles, tm), w_down, l, tiles, tm)
    return gather_rows(yg, slot, rows.tr), p1, p2


def kernel(x_prompt, x_sample, c_prompt, c_sample, state_rwkv_shift, state_rwkv_wkv, state_ret, state_gdn_conv,
           state_gdn, norm1_g, norm2_g, w_ada, b_ada, final_g, rwkv_mu, rwkv_w_rkv, rwkv_w0, rwkv_w1, rwkv_w2,
           rwkv_a0, rwkv_a1, rwkv_a2, rwkv_v0, rwkv_v1, rwkv_v2, rwkv_g1, rwkv_g2, rwkv_k_k, rwkv_k_a, rwkv_r_k,
           rwkv_ln_w, rwkv_ln_b, rwkv_w_o, ret_w_in, ret_w_o, gdn_w_in, gdn_conv_w, gdn_a_log, gdn_dt_bias,
           gdn_norm_g, gdn_w_o, ffn_w_gate, ffn_w_up, ffn_w_down, moe_w_router, moe_b_router, moe_w_gate,
           moe_w_up, moe_w_down):
    P = dict(rwkv_mu=rwkv_mu, rwkv_w_rkv=rwkv_w_rkv, rwkv_w0=rwkv_w0, rwkv_w1=rwkv_w1, rwkv_w2=rwkv_w2,
             rwkv_a0=rwkv_a0, rwkv_a1=rwkv_a1, rwkv_a2=rwkv_a2, rwkv_v0=rwkv_v0, rwkv_v1=rwkv_v1,
             rwkv_v2=rwkv_v2, rwkv_g1=rwkv_g1, rwkv_g2=rwkv_g2, rwkv_k_k=rwkv_k_k, rwkv_k_a=rwkv_k_a,
             rwkv_r_k=rwkv_r_k, rwkv_ln_w=rwkv_ln_w, rwkv_ln_b=rwkv_ln_b, rwkv_w_o=rwkv_w_o,
             ret_w_in=ret_w_in, ret_w_o=ret_w_o, gdn_w_in=gdn_w_in, gdn_conv_w=gdn_conv_w, gdn_a_log=gdn_a_log,
             gdn_dt_bias=gdn_dt_bias, gdn_norm_g=gdn_norm_g, gdn_w_o=gdn_w_o)
    b_p, L, D = x_prompt.shape
    b_s = x_sample.shape[0]
    depth = norm1_g.shape[0]
    rows = _Rows(b_p, L, b_s)
    t_p = rows.t_p
    x = jnp.concatenate([x_prompt.reshape(t_p, D), x_sample.reshape(b_s, D)], axis=0)
    c_act = jax.nn.silu(jnp.concatenate([c_prompt, c_sample], axis=0))
    mods = [(mm(c_act, w_ada, (i,)) + b_ada[i]).reshape(b_p + b_s, 6, D) for i in range(depth)]

    def mixer_input(i):
        if i % 3 != 0:
            return dict(h_dtypes=(BF16,))
        return dict(h_dtypes=(F32,), mix=(rwkv_mu[i // 3], state_rwkv_shift[i // 3]))

    h, *mixes = resid_norm(rows, x, norm1_g[0], scale=mods[0][:, 1], shift=mods[0][:, 0], **mixer_input(0))
    shifts, wkvs, rets, convs, gdns = [], [], [], [], []
    v_first = None
    for i in range(depth):
        j = i // 3
        m = mods[i]
        if i % 3 == 0:
            out, s_shift, s_wkv, v_first = _rwkv(rows, h, mixes, state_rwkv_wkv[j], v_first, P, j)
            shifts.append(s_shift)
            wkvs.append(s_wkv)
        elif i % 3 == 1:
            out, s_ret = _retention(rows, h, state_ret[j], P, j)
            rets.append(s_ret)
        else:
            out, s_conv, s_gdn = _gdn(rows, h, state_gdn_conv[j], state_gdn[j], P, j)
            convs.append(s_conv)
            gdns.append(s_gdn)
        x, h = resid_norm(rows, x, norm2_g[i], y=out, gate=m[:, 2], scale=m[:, 4], shift=m[:, 3],
                          h_dtypes=(BF16,) if i % 2 == 0 else (F32,))
        if i % 2 == 0:
            f = dict(y=mm(swiglu_up(h, ffn_w_gate, ffn_w_up, (i // 2,)), ffn_w_down, (i // 2,)))
        else:
            f = dict(pair=_moe(rows, h, i // 2, moe_w_router, moe_b_router, moe_w_gate, moe_w_up, moe_w_down))
        if i + 1 < depth:
            mn = mods[i + 1]
            x, h, *mixes = resid_norm(rows, x, norm1_g[i + 1], gate=m[:, 5], scale=mn[:, 1], shift=mn[:, 0],
                                      **mixer_input(i + 1), **f)
        else:
            x, h = resid_norm(rows, x, final_g, gate=m[:, 5], h_dtypes=(F32,), **f)
    y = h
    stack = lambda pairs, k, dt: jnp.stack([p[k] for p in pairs]).astype(dt)
    return (y[:t_p].reshape(b_p, L, D), y[t_p:].reshape(b_s, 1, D),
            stack(shifts, 0, state_rwkv_shift.dtype), stack(shifts, 1, state_rwkv_shift.dtype),
            stack(wkvs, 0, state_rwkv_wkv.dtype), stack(wkvs, 1, state_rwkv_wkv.dtype),
            stack(rets, 0, state_ret.dtype), stack(rets, 1, state_ret.dtype),
            stack(convs, 0, state_gdn_conv.dtype), stack(convs, 1, state_gdn_conv.dtype),
            stack(gdns, 0, state_gdn.dtype), stack(gdns, 1, state_gdn.dtype))
```
